```python
import jax, jax.numpy as jnp
from jax import lax
import numpy as np

D_MODEL = 1024
BATCH = 4
SEQ = 8192
DEPTH = 2

N_META = 16
GLA_HEADS = 4
GLA_DK = D_MODEL // 2
GLA_DV = D_MODEL
GLA_HK = GLA_DK // GLA_HEADS
GLA_HV = GLA_DV // GLA_HEADS
GLA_RANK = 16
GLA_TAU = 16.0
CHUNK = 64
POOL_WINDOWS = (2, 4, 8, 16)
POOL_GROUPS = 4
POOL_DIM = D_MODEL
POOL_GDIM = POOL_DIM // POOL_GROUPS
D_FF = 2816
CONV_W = 3
EPS = 1e-6
IN_WIDTH = 2 * GLA_DK + 2 * GLA_DV + GLA_RANK + POOL_DIM + 2 * D_MODEL

kernel_name = "hybrid_gla_pool_gated_block"


def rmsnorm(x, g):
    xf = x.astype(jnp.float32)
    y = xf * lax.rsqrt(jnp.mean(xf * xf, axis=-1, keepdims=True) + EPS)
    return (y * g.astype(jnp.float32)).astype(x.dtype)


def gla_chunked(q, k, v, log_a):
    B, L, H, DKH = q.shape
    DVH = v.shape[-1]
    pad = CHUNK - N_META
    n_chunks = (L + pad) // CHUNK

    def to_chunks(t):
        t = jnp.pad(t.astype(jnp.float32), ((0, 0), (pad, 0), (0, 0), (0, 0)))
        return t.reshape(B, n_chunks, CHUNK, H, t.shape[-1]).transpose(1, 0, 3, 2, 4)

    q, k, v, g = (to_chunks(t) for t in (q, k, v, log_a))
    b = jnp.cumsum(g, axis=3)
    b_last = b[:, :, :, -1:, :]
    q_dec = q * jnp.exp(b)
    k_inv = k * jnp.exp(-b)
    k_end = k * jnp.exp(b_last - b)
    causal = jnp.tril(jnp.ones((CHUNK, CHUNK), dtype=bool))
    att = jnp.where(causal, jnp.einsum('nbhcd,nbhsd->nbhcs', q_dec, k_inv), 0.0)
    o_intra = jnp.einsum('nbhcs,nbhse->nbhce', att, v)

    def step(state, inp):
        q_c, k_c, v_c, dec_c = inp
        o_c = jnp.einsum('bhcd,bhde->bhce', q_c, state)
        state = dec_c[..., None] * state + jnp.einsum('bhsd,bhse->bhde', k_c, v_c)
        return state, o_c

    s0 = jnp.zeros((B, H, DKH, DVH), jnp.float32)
    _, o_inter = lax.scan(step, s0, (q_dec, k_end, v, jnp.exp(b_last[:, :, :, 0, :])))
    o = (o_intra + o_inter).transpose(1, 0, 3, 2, 4).reshape(B, n_chunks * CHUNK, H, DVH)
    return o[:, pad:]


def multiscale_pool(u):
    B, L, _ = u.shape
    ug = u.astype(jnp.float32).reshape(B, L, POOL_GROUPS, POOL_GDIM)
    csp = jnp.pad(jnp.cumsum(ug, axis=1), ((0, 0), (1, 0), (0, 0), (0, 0)))
    pos = jnp.arange(L)
    outs = []
    for gi, w in enumerate(POOL_WINDOWS):
        c = csp[:, :, gi]
        lagged = jnp.pad(c[:, :L - w + 1], ((0, 0), (w - 1, 0), (0, 0)))
        cnt = jnp.minimum(pos + 1, w).astype(jnp.float32)[None, :, None]
        outs.append((c[:, 1:] - lagged) / cnt)
    return jnp.stack(outs, axis=2) - ug


def causal_dwconv(h, w, b):
    C = h.shape[-1]
    out = lax.conv_general_dilated(h, w[:, None, :].astype(h.dtype), window_strides=(1,),
                                   padding=((CONV_W - 1, 0),),
                                   dimension_numbers=('NWC', 'WIO', 'NWC'),
                                   feature_group_count=C)
    return out + b


def setup_inputs(seed: int = 0) -> dict:
    key = jax.random.key(seed)
    ks = jax.random.split(key, 20)
    nrm = lambda k, shape, s: jax.random.normal(k, shape, jnp.float32) * s
    F2 = 2 * D_FF
    return {
        'x': nrm(ks[0], (BATCH, SEQ, D_MODEL), 1.0),
        'meta_tokens': nrm(ks[1], (N_META, D_MODEL), 1.0),
        'norm1_g': 1.0 + nrm(ks[2], (DEPTH, D_MODEL), 0.02),
        'w_in': nrm(ks[3], (DEPTH, D_MODEL, IN_WIDTH), D_MODEL ** -0.5),
        'w_gk': nrm(ks[4], (DEPTH, GLA_RANK, GLA_DK), GLA_RANK ** -0.5),
        'b_gk': nrm(ks[5], (DEPTH, GLA_DK), 0.1),
        'gla_norm_g': 1.0 + nrm(ks[6], (DEPTH, GLA_HV), 0.02),
        'w_a': nrm(ks[7], (DEPTH, GLA_DV, D_MODEL), GLA_DV ** -0.5),
        'w_pool_grp': nrm(ks[8], (DEPTH, POOL_GROUPS, POOL_GDIM, POOL_GDIM), POOL_GDIM ** -0.5),
        'pool_scale': 1.0 + nrm(ks[9], (DEPTH, POOL_DIM), 0.02),
        'w_b': nrm(ks[10], (DEPTH, POOL_DIM, D_MODEL), POOL_DIM ** -0.5),
        'b_gates': nrm(ks[11], (DEPTH, 2 * D_MODEL), 0.02),
        'w_o': nrm(ks[12], (DEPTH, D_MODEL, D_MODEL), D_MODEL ** -0.5),
        'norm2_g': 1.0 + nrm(ks[13], (DEPTH, D_MODEL), 0.02),
        'w_up': nrm(ks[14], (DEPTH, D_MODEL, F2), D_MODEL ** -0.5),
        'conv_w': nrm(ks[15], (DEPTH, CONV_W, F2), CONV_W ** -0.5),
        'conv_b': nrm(ks[16], (DEPTH, F2), 0.02),
        'w_down': nrm(ks[17], (DEPTH, D_FF, D_MODEL), D_FF ** -0.5),
        'final_norm_g': 1.0 + nrm(ks[18], (D_MODEL,), 0.02),
    }


def reference(x, meta_tokens, norm1_g, w_in, w_gk, b_gk, gla_norm_g, w_a, w_pool_grp, pool_scale,
              w_b, b_gates, w_o, norm2_g, w_up, conv_w, conv_b, w_down, final_norm_g):
    B = x.shape[0]
    dt = x.dtype
    meta = jnp.broadcast_to(meta_tokens.astype(dt)[None], (B, N_META, D_MODEL))
    h = jnp.concatenate([meta, x], axis=1)
    L = h.shape[1]
    sizes = (GLA_DK, GLA_DK, GLA_DV, GLA_RANK, GLA_DV, POOL_DIM, D_MODEL, D_MODEL)
    splits = np.cumsum(sizes)[:-1].tolist()

    for l in range(DEPTH):
        hn = rmsnorm(h, norm1_g[l])
        p = hn @ w_in[l]
        q, k, v, glr, r, u, ga, gb = jnp.split(p, splits, axis=-1)
        log_a = jax.nn.log_sigmoid((glr @ w_gk[l] + b_gk[l]).astype(jnp.float32)) / GLA_TAU
        q = q.reshape(B, L, GLA_HEADS, GLA_HK) * (GLA_HK ** -0.5)
        k = k.reshape(B, L, GLA_HEADS, GLA_HK)
        v = v.reshape(B, L, GLA_HEADS, GLA_HV)
        log_a = log_a.reshape(B, L, GLA_HEADS, GLA_HK)
        o = gla_chunked(q, k, v, log_a)
        o = rmsnorm(o, gla_norm_g[l]).reshape(B, L, GLA_DV).astype(dt)
        y_a = (o * jax.nn.silu(r)) @ w_a[l]
        pooled = multiscale_pool(u).astype(dt)
        y_b = jnp.einsum('blgc,gcd->blgd', pooled, w_pool_grp[l]).reshape(B, L, POOL_DIM)
        y_b = (y_b * pool_scale[l]) @ w_b[l]
        gate_a = jax.nn.sigmoid(ga + b_gates[l, :D_MODEL])
        gate_b = jax.nn.sigmoid(gb + b_gates[l, D_MODEL:])
        h = h + (gate_a * y_a + gate_b * y_b) @ w_o[l]
        hn = rmsnorm(h, norm2_g[l])
        up = causal_dwconv(hn @ w_up[l], conv_w[l], conv_b[l])
        a, bv = jnp.split(up, 2, axis=-1)
        h = h + (jax.nn.silu(a) * bv) @ w_down[l]

    return rmsnorm(h, final_norm_g)[:, N_META:]
```

```python
import functools

import jax
import jax.numpy as jnp
from jax import lax
from jax.experimental import pallas as pl
from jax.experimental.pallas import tpu as pltpu

F32 = jnp.float32
BF16 = jnp.bfloat16

D_MODEL = 1024
N_META = 16
GLA_HEADS = 4
GLA_HK = 128
GLA_HV = 256
GLA_DK = GLA_HEADS * GLA_HK
GLA_DV = GLA_HEADS * GLA_HV
GLA_RANK = 16
GLA_TAU = 16.0
CHUNK = 64
POOL_WINDOWS = (2, 4, 8, 16)
POOL_GDIM = 256
D_FF = 2816
CONV_W = 3
EPS = 1e-6

PAD = 240
HALO = 16
TM = 768
FF_CHUNK = 256
LANES = 128
VMEM_LIMIT = 60 * 1024 * 1024

_Q0, _K0, _V0, _R0, _U0, _GA0, _GB0, _GLR0 = 0, 512, 1024, 2048, 3072, 4096, 5120, 6144
IN_COLS = _GLR0 + LANES


def _rms(x, g):
    ms = jnp.mean(x * x, axis=-1, keepdims=True)
    return x * lax.rsqrt(ms + EPS) * g


def _row_in_seq(tile_idx, tiles_per_seq, shape):
    base = (tile_idx % tiles_per_seq) * shape[0]
    return base + lax.broadcasted_iota(jnp.int32, shape, 0)


def _const_spec(shape):
    nd = len(shape)
    return pl.BlockSpec(shape, lambda *_: (0,) * nd, pipeline_mode=pl.Buffered(1))


def _inproj_kernel(tiles_per_seq, h_ref, g1_ref, w_ref, wgk_ref, bgk_ref, bg_ref,
                   q_ref, k_ref, v_ref, g_ref, sr_ref, pooled_ref, ga_ref, gb_ref, carry_ref):
    i = pl.program_id(0)
    tm = h_ref.shape[0]

    @pl.when(i == 0)
    def _():
        carry_ref[...] = jnp.zeros_like(carry_ref)

    hn = _rms(h_ref[...], g1_ref[...]).astype(BF16)

    def proj(c0, c1):
        return jnp.dot(hn, w_ref[:, c0:c1], preferred_element_type=F32)

    q_ref[...] = (proj(_Q0, _K0) * (GLA_HK ** -0.5)).astype(BF16)
    k_ref[...] = proj(_K0, _V0).astype(BF16)
    v_ref[...] = proj(_V0, _R0).astype(BF16)
    r = proj(_R0, _U0)
    sr_ref[...] = (r * jax.nn.sigmoid(r)).astype(BF16)
    ga_ref[...] = jax.nn.sigmoid(proj(_GA0, _GB0) + bg_ref[:, :D_MODEL]).astype(BF16)
    gb_ref[...] = jax.nn.sigmoid(proj(_GB0, _GLR0) + bg_ref[:, D_MODEL:]).astype(BF16)

    glr = proj(_GLR0, IN_COLS).astype(BF16)
    z = jnp.dot(glr, wgk_ref[...], preferred_element_type=F32) + bgk_ref[...]
    g_ref[...] = (jnp.minimum(z, 0.0) - jnp.log1p(jnp.exp(-jnp.abs(z)))) * (1.0 / GLA_TAU)

    u = proj(_U0, _GA0)
    ext = jnp.concatenate([carry_ref[...], u], axis=0)
    carry_ref[...] = u[tm - HALO:, :]
    pos = _row_in_seq(i, tiles_per_seq, (tm, POOL_GDIM)) - PAD
    for gi, w in enumerate(POOL_WINDOWS):
        cols = slice(gi * POOL_GDIM, (gi + 1) * POOL_GDIM)
        s = ext[:, cols]
        sh = 1
        while sh < w:
            s = s + pltpu.roll(s, sh, axis=0)
            sh *= 2
        cnt = jnp.clip(pos + 1, 1, w).astype(F32)
        pooled_ref[:, cols] = (s[HALO:, :] / cnt - u[:, cols]).astype(BF16)


def _inproj(h, g1, w_r, wgk_pad, bgk, bgates, tiles_per_seq):
    ntok = h.shape[0]
    tok = lambda n: pl.BlockSpec((TM, n), lambda i: (i, 0))
    out_shape = (
        jax.ShapeDtypeStruct((ntok, GLA_DK), BF16),
        jax.ShapeDtypeStruct((ntok, GLA_DK), BF16),
        jax.ShapeDtypeStruct((ntok, GLA_DV), BF16),
        jax.ShapeDtypeStruct((ntok, GLA_DK), F32),
        jax.ShapeDtypeStruct((ntok, GLA_DV), BF16),
        jax.ShapeDtypeStruct((ntok, D_MODEL), BF16),
        jax.ShapeDtypeStruct((ntok, D_MODEL), BF16),
        jax.ShapeDtypeStruct((ntok, D_MODEL), BF16),
    )
    return pl.pallas_call(
        functools.partial(_inproj_kernel, tiles_per_seq),
        out_shape=out_shape,
        grid=(ntok // TM,),
        in_specs=[tok(D_MODEL), _const_spec((1, D_MODEL)), _const_spec((D_MODEL, IN_COLS)),
                  _const_spec((LANES, GLA_DK)), _const_spec((1, GLA_DK)), _const_spec((1, 2 * D_MODEL))],
        out_specs=(tok(GLA_DK), tok(GLA_DK), tok(GLA_DV), tok(GLA_DK), tok(GLA_DV),
                   tok(D_MODEL), tok(D_MODEL), tok(D_MODEL)),
        scratch_shapes=[pltpu.VMEM((HALO, D_MODEL), F32)],
        compiler_params=pltpu.CompilerParams(dimension_semantics=("arbitrary",),
                                             vmem_limit_bytes=VMEM_LIMIT),
        name="inproj",
    )(h, g1, w_r, wgk_pad, bgk, bgates)


def _gla_kernel(q_ref, k_ref, v_ref, g_ref, sr_ref, gn_ref, y_ref, state_ref, o_ref):
    t = pl.program_id(1)
    tg = q_ref.shape[0]

    @pl.when(t == 0)
    def _():
        state_ref[...] = jnp.zeros_like(state_ref)

    rr = lax.broadcasted_iota(jnp.int32, (CHUNK, CHUNK), 0)
    cc = lax.broadcasted_iota(jnp.int32, (CHUNK, CHUNK), 1)
    causal = rr >= cc
    tril = causal.astype(BF16)

    def chunk_body(c, carry):
        rows = pl.ds(pl.multiple_of(c * CHUNK, CHUNK), CHUNK)
        g = g_ref[rows, :]
        g_hi = g.astype(BF16)
        g_lo = (g - g_hi.astype(F32)).astype(BF16)
        b = (jnp.dot(tril, g_hi, preferred_element_type=F32)
             + jnp.dot(tril, g_lo, preferred_element_type=F32))
        b_last = b[CHUNK - 1:CHUNK, :]
        q = q_ref[rows, :].astype(F32)
        k = k_ref[rows, :].astype(F32)
        q_dec = (q * jnp.exp(b)).astype(BF16)
        k_inv = (k * jnp.exp(-b)).astype(BF16)
        k_end = (k * jnp.exp(b_last - b)).astype(BF16)
        dec = jnp.exp(b_last)
        for h in range(GLA_HEADS):
            ks = slice(h * GLA_HK, (h + 1) * GLA_HK)
            vs = slice(h * GLA_HV, (h + 1) * GLA_HV)
            vh = v_ref[rows, vs]
            att = lax.dot_general(q_dec[:, ks], k_inv[:, ks], (((1,), (1,)), ((), ())),
                                  preferred_element_type=F32)
            att = jnp.where(causal, att, 0.0).astype(BF16)
            state = state_ref[h]
            o_ref[rows, vs] = (jnp.dot(att, vh, preferred_element_type=F32)
                               + jnp.dot(q_dec[:, ks], state.astype(BF16), preferred_element_type=F32))
            d_state = lax.dot_general(k_end[:, ks], vh, (((0,), (0,)), ((), ())),
                                      preferred_element_type=F32)
            dec_col = jnp.broadcast_to(dec[:, ks], (GLA_HK, GLA_HK)).T
            state_ref[h] = jnp.concatenate([dec_col, dec_col], axis=1) * state + d_state
        return carry

    lax.fori_loop(0, tg // CHUNK, chunk_body, 0)

    for h in range(GLA_HEADS):
        vs = slice(h * GLA_HV, (h + 1) * GLA_HV)
        y_ref[:, vs] = (_rms(o_ref[:, vs], gn_ref[...]) * sr_ref[:, vs].astype(F32)).astype(BF16)


def _gla(q, k, v, g, sr, gn, batch, tiles_per_seq):
    ntok = q.shape[0]
    tok = lambda n: pl.BlockSpec((TM, n), lambda b, t: (b * tiles_per_seq + t, 0))
    return pl.pallas_call(
        _gla_kernel,
        out_shape=jax.ShapeDtypeStruct((ntok, GLA_DV), BF16),
        grid=(batch, tiles_per_seq),
        in_specs=[tok(GLA_DK), tok(GLA_DK), tok(GLA_DV), tok(GLA_DK), tok(GLA_DV),
                  pl.BlockSpec((1, GLA_HV), lambda b, t: (0, 0))],
        out_specs=tok(GLA_DV),
        scratch_shapes=[pltpu.VMEM((GLA_HEADS, GLA_HK, GLA_HV), F32), pltpu.VMEM((TM, GLA_DV), F32)],
        compiler_params=pltpu.CompilerParams(dimension_semantics=("arbitrary", "arbitrary"),
                                             vmem_limit_bytes=VMEM_LIMIT),
        name="gla",
    )(q, k, v, g, sr, gn)


def _merge_kernel(tiles_per_seq, ya_ref, pooled_ref, ga_ref, gb_ref, h_ref,
                  wa_ref, wp_ref, ps_ref, wb_ref, wo_ref, out_ref):
    i = pl.program_id(0)
    y_a = jnp.dot(ya_ref[...], wa_ref[...], preferred_element_type=F32)
    parts = [jnp.dot(pooled_ref[:, gi * POOL_GDIM:(gi + 1) * POOL_GDIM], wp_ref[gi],
                     preferred_element_type=F32) for gi in range(len(POOL_WINDOWS))]
    yb_in = (jnp.concatenate(parts, axis=1) * ps_ref[...]).astype(BF16)
    y_b = jnp.dot(yb_in, wb_ref[...], preferred_element_type=F32)
    m = (ga_ref[...].astype(F32) * y_a + gb_ref[...].astype(F32) * y_b).astype(BF16)
    h_new = h_ref[...] + jnp.dot(m, wo_ref[...], preferred_element_type=F32)
    row = _row_in_seq(i, tiles_per_seq, h_new.shape)
    out_ref[...] = jnp.where(row >= PAD, h_new, 0.0)


def _merge(ya, pooled, ga, gb, h, wa, wp, ps, wb, wo, tiles_per_seq):
    ntok = h.shape[0]
    tok = pl.BlockSpec((TM, D_MODEL), lambda i: (i, 0))
    sq = _const_spec((D_MODEL, D_MODEL))
    return pl.pallas_call(
        functools.partial(_merge_kernel, tiles_per_seq),
        out_shape=jax.ShapeDtypeStruct((ntok, D_MODEL), F32),
        grid=(ntok // TM,),
        in_specs=[tok, tok, tok, tok, tok, sq,
                  _const_spec((len(POOL_WINDOWS), POOL_GDIM, POOL_GDIM)), _const_spec((1, D_MODEL)), sq, sq],
        out_specs=tok,
        compiler_params=pltpu.CompilerParams(dimension_semantics=("arbitrary",),
                                             vmem_limit_bytes=VMEM_LIMIT),
        name="merge",
    )(ya, pooled, ga, gb, h, wa, wp, ps, wb, wo)


def _ffn_kernel(tiles_per_seq, h_ref, halo_ref, g2_ref, wup_ref, cw_ref, cb_ref, wdn_ref, out_ref):
    i = pl.program_id(0)
    tm = h_ref.shape[0]
    h = h_ref[...]
    hn = _rms(h, g2_ref[...]).astype(BF16)
    hn_halo = _rms(halo_ref[...], g2_ref[...]).astype(BF16)
    hn_ext = jnp.concatenate([hn_halo, hn], axis=0)
    acc = h
    for j in range(D_FF // FF_CHUNK):
        cols = slice(2 * j * FF_CHUNK, 2 * (j + 1) * FF_CHUNK)
        up = jnp.dot(hn_ext, wup_ref[:, cols], preferred_element_type=F32)
        c = cb_ref[:, cols]
        for tap in range(CONV_W):
            lo = HALO - (CONV_W - 1) + tap
            c = c + cw_ref[tap:tap + 1, cols] * up[lo:lo + tm, :]
        a, bv = c[:, :FF_CHUNK], c[:, FF_CHUNK:]
        act = (a * jax.nn.sigmoid(a) * bv).astype(BF16)
        acc = acc + jnp.dot(act, wdn_ref[j * FF_CHUNK:(j + 1) * FF_CHUNK, :], preferred_element_type=F32)
    row = _row_in_seq(i, tiles_per_seq, acc.shape)
    out_ref[...] = jnp.where(row >= PAD, acc, 0.0)


def _ffn(h, g2, wup_r, cw_r, cb_r, wdn, tiles_per_seq):
    ntok = h.shape[0]
    tok = pl.BlockSpec((TM, D_MODEL), lambda i: (i, 0))
    halo = pl.BlockSpec((HALO, D_MODEL), lambda i: (jnp.maximum(i * (TM // HALO) - 1, 0), 0))
    return pl.pallas_call(
        functools.partial(_ffn_kernel, tiles_per_seq),
        out_shape=jax.ShapeDtypeStruct((ntok, D_MODEL), F32),
        grid=(ntok // TM,),
        in_specs=[tok, halo, _const_spec((1, D_MODEL)), _const_spec((D_MODEL, 2 * D_FF)),
                  _const_spec((CONV_W, 2 * D_FF)), _const_spec((1, 2 * D_FF)), _const_spec((D_FF, D_MODEL))],
        out_specs=tok,
        compiler_params=pltpu.CompilerParams(dimension_semantics=("arbitrary",),
                                             vmem_limit_bytes=VMEM_LIMIT),
        name="ffn",
    )(h, h, g2, wup_r, cw_r, cb_r, wdn)


def _final_kernel(h_ref, g_ref, out_ref):
    out_ref[...] = _rms(h_ref[...], g_ref[...])


def _final_norm(h, g, batch, seq):
    rows = PAD + N_META
    in_per_seq = (rows + seq) // rows
    out_per_seq = seq // rows
    return pl.pallas_call(
        _final_kernel,
        out_shape=jax.ShapeDtypeStruct((batch * seq, D_MODEL), F32),
        grid=(batch, out_per_seq),
        in_specs=[pl.BlockSpec((rows, D_MODEL), lambda b, t: (b * in_per_seq + t + 1, 0)),
                  pl.BlockSpec((1, D_MODEL), lambda b, t: (0, 0))],
        out_specs=pl.BlockSpec((rows, D_MODEL), lambda b, t: (b * out_per_seq + t, 0)),
        compiler_params=pltpu.CompilerParams(dimension_semantics=("arbitrary", "arbitrary")),
        name="final_norm",
    )(h, g)


def _regroup_ffn(w):
    lead = w.shape[:-1]
    n = D_FF // FF_CHUNK
    w = w.reshape(lead + (2, n, FF_CHUNK))
    w = jnp.swapaxes(w, -3, -2)
    return w.reshape(lead + (2 * D_FF,))


def kernel(x, meta_tokens, norm1_g, w_in, w_gk, b_gk, gla_norm_g, w_a, w_pool_grp, pool_scale,
           w_b, b_gates, w_o, norm2_g, w_up, conv_w, conv_b, w_down, final_norm_g):
    batch, seq, d = x.shape
    depth = w_in.shape[0]
    lp = PAD + N_META + seq
    assert d == D_MODEL and lp % TM == 0 and (PAD + N_META) % CHUNK == 0 and seq % (PAD + N_META) == 0
    tiles_per_seq = lp // TM

    meta = jnp.broadcast_to(meta_tokens.astype(F32)[None], (batch, N_META, d))
    h = jnp.concatenate([jnp.zeros((batch, PAD, d), F32), meta, x.astype(F32)], axis=1)
    h = h.reshape(batch * lp, d)

    glr0 = 2 * GLA_DK + GLA_DV
    for l in range(depth):
        wl = w_in[l]
        w_r = jnp.concatenate(
            [wl[:, :glr0], wl[:, glr0 + GLA_RANK:], wl[:, glr0:glr0 + GLA_RANK],
             jnp.zeros((d, LANES - GLA_RANK), wl.dtype)], axis=1).astype(BF16)
        wgk_pad = jnp.concatenate(
            [w_gk[l], jnp.zeros((LANES - GLA_RANK, GLA_DK), w_gk.dtype)], axis=0).astype(BF16)
        q, k, v, g, sr, pooled, ga, gb = _inproj(
            h, norm1_g[l][None, :], w_r, wgk_pad, b_gk[l][None, :], b_gates[l][None, :], tiles_per_seq)
        ya = _gla(q, k, v, g, sr, gla_norm_g[l][None, :], batch, tiles_per_seq)
        h = _merge(ya, pooled, ga, gb, h, w_a[l].astype(BF16), w_pool_grp[l].astype(BF16),
                   pool_scale[l][None, :], w_b[l].astype(BF16), w_o[l].astype(BF16), tiles_per_seq)
        h = _ffn(h, norm2_g[l][None, :], _regroup_ffn(w_up[l]).astype(BF16), _regroup_ffn(conv_w[l]),
                 _regroup_ffn(conv_b[l])[None, :], w_down[l].astype(BF16), tiles_per_seq)

    out = _final_norm(h, final_norm_g[None, :], batch, seq)
    return out.reshape(batch, seq, d)
```

```python
import functools

import jax
import jax.numpy as jnp
from jax import lax
from jax.experimental import pallas as pl
from jax.experimental.pallas import tpu as pltpu

F32 = jnp.float32
BF16 = jnp.bfloat16

D_MODEL = 1024
N_META = 16
GLA_HEADS = 4
GLA_HK = 128
GLA_HV = 256
GLA_DK = GLA_HEADS * GLA_HK
GLA_DV = GLA_HEADS * GLA_HV
GLA_RANK = 16
GLA_TAU = 16.0
CHUNK = 64
POOL_WINDOWS = (2, 4, 8, 16)
POOL_GDIM = 256
D_FF = 2816
CONV_W = 3
EPS = 1e-6

PAD = 240
HALO = 16
TM = 768
INPROJ_COLS = 512
FF_CHUNK = 256
FF_GROUPS = (4, 4, 2, 1)
LANES = 128
VMEM_LIMIT = 60 * 1024 * 1024

_Q0, _K0, _V0, _R0, _U0, _GA0, _GB0, _GLR0 = 0, 512, 1024, 2048, 3072, 4096, 5120, 6144
IN_COLS = _GLR0 + LANES


def _rms(x, g):
    ms = jnp.mean(x * x, axis=-1, keepdims=True)
    return x * lax.rsqrt(ms + EPS) * g


def _row_in_seq(tile_idx, tiles_per_seq, shape):
    base = (tile_idx % tiles_per_seq) * shape[0]
    return base + lax.broadcasted_iota(jnp.int32, shape, 0)


def _const_spec(shape):
    nd = len(shape)
    return pl.BlockSpec(shape, lambda *_: (0,) * nd, pipeline_mode=pl.Buffered(1))


def _inproj_kernel(tiles_per_seq, h_ref, g1_ref, w_ref, wgk_ref, bgk_ref, bg_ref,
                   q_ref, k_ref, v_ref, g_ref, sr_ref, pooled_ref, ga_ref, gb_ref, carry_ref):
    i = pl.program_id(0)
    tm = h_ref.shape[0]

    @pl.when(i == 0)
    def _():
        carry_ref[...] = jnp.zeros_like(carry_ref)

    hn = _rms(h_ref[...], g1_ref[...]).astype(BF16)

    half = INPROJ_COLS
    head_pos = (i % tiles_per_seq) * tm + lax.broadcasted_iota(jnp.int32, (HALO, POOL_GDIM), 0)
    stash = {}

    def proj(c0, c1):
        return lambda: jnp.dot(hn, w_ref[:, c0:c1], preferred_element_type=F32)

    def pool_epilogue(first_group):
        def fn(u):
            c0 = first_group * POOL_GDIM
            ext = jnp.concatenate([carry_ref[:, c0:c0 + half], u], axis=0)
            carry_ref[:, c0:c0 + half] = u[tm - HALO:, :]
            for gl in range(half // POOL_GDIM):
                w = POOL_WINDOWS[first_group + gl]
                loc = slice(gl * POOL_GDIM, (gl + 1) * POOL_GDIM)
                s = ext[:, loc]
                sh = 1
                while sh < w:
                    s = s + pltpu.roll(s, sh, axis=0)
                    sh *= 2
                out_cols = slice(c0 + gl * POOL_GDIM, c0 + (gl + 1) * POOL_GDIM)
                mean = s[HALO:, :] * (1.0 / w)
                pooled_ref[:, out_cols] = (mean - u[:, loc]).astype(BF16)
                cnt = jnp.clip(head_pos + 1, 1, w).astype(F32)
                pooled_ref[PAD:PAD + HALO, out_cols] = (
                    s[HALO + PAD:2 * HALO + PAD, :] / cnt - u[PAD:PAD + HALO, loc]).astype(BF16)
        return fn

    def glr_epilogue(x):
        stash["glr"] = x.astype(BF16)

    def forget_gate_epilogue(z):
        z = z + bgk_ref[...]
        g_ref[...] = (jnp.minimum(z, 0.0) - jnp.log(1.0 + jnp.exp(-jnp.abs(z)))) * (1.0 / GLA_TAU)

    def gate_epilogue(out_ref, bias0, c0):
        def fn(x):
            out_ref[:, c0:c0 + half] = jax.nn.sigmoid(x + bg_ref[:, bias0 + c0:bias0 + c0 + half]).astype(BF16)
        return fn

    def silu_epilogue(c0):
        def fn(x):
            sr_ref[:, c0:c0 + half] = (x * jax.nn.sigmoid(x)).astype(BF16)
        return fn

    def cast_epilogue(out_ref, c0, scale=None):
        def fn(x):
            out_ref[:, c0:c0 + half] = (x if scale is None else x * scale).astype(BF16)
        return fn

    assert 2 * half == D_MODEL == GLA_DV and half == GLA_DK
    gate = lambda ref, col0, bias0, c0: (proj(col0 + c0, col0 + c0 + half), gate_epilogue(ref, bias0, c0))
    stages = [
        (proj(_GLR0, IN_COLS), glr_epilogue),
        (proj(_U0 + half, _GA0), pool_epilogue(half // POOL_GDIM)),
        gate(ga_ref, _GA0, 0, 0),
        (lambda: jnp.dot(stash["glr"], wgk_ref[...], preferred_element_type=F32), forget_gate_epilogue),
        gate(ga_ref, _GA0, 0, half),
        (proj(_U0, _U0 + half), pool_epilogue(0)),
        gate(gb_ref, _GB0, D_MODEL, 0),
        (proj(_R0, _R0 + half), silu_epilogue(0)),
        gate(gb_ref, _GB0, D_MODEL, half),
        (proj(_R0 + half, _U0), silu_epilogue(half)),
        (proj(_V0, _V0 + half), cast_epilogue(v_ref, 0)),
        (proj(_V0 + half, _R0), cast_epilogue(v_ref, half)),
        (proj(_K0, _V0), cast_epilogue(k_ref, 0)),
        (proj(_Q0, _K0), cast_epilogue(q_ref, 0, GLA_HK ** -0.5)),
    ]

    res = stages[0][0]()
    for idx, (_, epilogue) in enumerate(stages):
        nxt = stages[idx + 1][0]() if idx + 1 < len(stages) else None
        epilogue(res)
        res = nxt


def _inproj(h, g1, w_r, wgk_pad, bgk, bgates, tiles_per_seq):
    ntok = h.shape[0]
    tok = lambda n: pl.BlockSpec((TM, n), lambda i: (i, 0))
    out_shape = (
        jax.ShapeDtypeStruct((ntok, GLA_DK), BF16),
        jax.ShapeDtypeStruct((ntok, GLA_DK), BF16),
        jax.ShapeDtypeStruct((ntok, GLA_DV), BF16),
        jax.ShapeDtypeStruct((ntok, GLA_DK), F32),
        jax.ShapeDtypeStruct((ntok, GLA_DV), BF16),
        jax.ShapeDtypeStruct((ntok, D_MODEL), BF16),
        jax.ShapeDtypeStruct((ntok, D_MODEL), BF16),
        jax.ShapeDtypeStruct((ntok, D_MODEL), BF16),
    )
    return pl.pallas_call(
        functools.partial(_inproj_kernel, tiles_per_seq),
        out_shape=out_shape,
        grid=(ntok // TM,),
        in_specs=[tok(D_MODEL), _const_spec((1, D_MODEL)), _const_spec((D_MODEL, IN_COLS)),
                  _const_spec((LANES, GLA_DK)), _const_spec((1, GLA_DK)), _const_spec((1, 2 * D_MODEL))],
        out_specs=(tok(GLA_DK), tok(GLA_DK), tok(GLA_DV), tok(GLA_DK), tok(GLA_DV),
                   tok(D_MODEL), tok(D_MODEL), tok(D_MODEL)),
        scratch_shapes=[pltpu.VMEM((HALO, D_MODEL), F32)],
        compiler_params=pltpu.CompilerParams(dimension_semantics=("arbitrary",),
                                             vmem_limit_bytes=VMEM_LIMIT),
        name="inproj",
    )(h, g1, w_r, wgk_pad, bgk, bgates)


def _gla_kernel(q_ref, k_ref, v_ref, g_ref, sr_ref, gn_ref, y_ref, state_ref, o_ref):
    t = pl.program_id(1)
    tg = q_ref.shape[0]

    @pl.when(t == 0)
    def _():
        state_ref[...] = jnp.zeros_like(state_ref)

    rr = lax.broadcasted_iota(jnp.int32, (CHUNK, CHUNK), 0)
    cc = lax.broadcasted_iota(jnp.int32, (CHUNK, CHUNK), 1)
    causal = rr >= cc
    tril = causal.astype(BF16)

    n_chunks = tg // CHUNK
    units = [(c, h) for c in range(n_chunks) for h in range(GLA_HEADS)]
    rows = lambda c: slice(c * CHUNK, (c + 1) * CHUNK)
    kcols = lambda h: slice(h * GLA_HK, (h + 1) * GLA_HK)
    vcols = lambda h: slice(h * GLA_HV, (h + 1) * GLA_HV)

    g = g_ref[...]
    g_hi = g.astype(BF16)
    g_lo = (g - g_hi.astype(F32)).astype(BF16)
    b_chunks = [jnp.dot(tril, g_hi[rows(c)], preferred_element_type=F32)
                + jnp.dot(tril, g_lo[rows(c)], preferred_element_type=F32) for c in range(n_chunks)]
    b = jnp.concatenate(b_chunks, axis=0)
    b_last = [bc[CHUNK - 1:CHUNK, :] for bc in b_chunks]
    b_end = jnp.concatenate([jnp.broadcast_to(bl, (CHUNK, GLA_DK)) for bl in b_last], axis=0)
    q = q_ref[...].astype(F32)
    k = k_ref[...].astype(F32)
    q_dec = (q * jnp.exp(b)).astype(BF16)
    k_inv = (k * jnp.exp(-b)).astype(BF16)
    k_end = (k * jnp.exp(b_end - b)).astype(BF16)
    dec = [jnp.exp(bl) for bl in b_last]

    att = {}
    for c, h in units:
        a = lax.dot_general(q_dec[rows(c), kcols(h)], k_inv[rows(c), kcols(h)], (((1,), (1,)), ((), ())),
                            preferred_element_type=F32)
        att[c, h] = jnp.where(causal, a, 0.0).astype(BF16)
    for c, h in units:
        o_ref[rows(c), vcols(h)] = jnp.dot(att[c, h], v_ref[rows(c), vcols(h)], preferred_element_type=F32)
    d_state = {}
    for c, h in units:
        d_state[c, h] = lax.dot_general(k_end[rows(c), kcols(h)], v_ref[rows(c), vcols(h)],
                                        (((0,), (0,)), ((), ())), preferred_element_type=F32)

    state = [state_ref[h] for h in range(GLA_HEADS)]
    for c, h in units:
        o_ref[rows(c), vcols(h)] += jnp.dot(q_dec[rows(c), kcols(h)], state[h].astype(BF16),
                                            preferred_element_type=F32)
        dec_col = jnp.broadcast_to(dec[c][:, kcols(h)], (GLA_HK, GLA_HK)).T
        state[h] = jnp.concatenate([dec_col, dec_col], axis=1) * state[h] + d_state[c, h]
    for h in range(GLA_HEADS):
        state_ref[h] = state[h]

    for h in range(GLA_HEADS):
        vs = slice(h * GLA_HV, (h + 1) * GLA_HV)
        y_ref[:, vs] = (_rms(o_ref[:, vs], gn_ref[...]) * sr_ref[:, vs].astype(F32)).astype(BF16)


def _gla(q, k, v, g, sr, gn, batch, tiles_per_seq):
    ntok = q.shape[0]
    tok = lambda n: pl.BlockSpec((TM, n), lambda b, t: (b * tiles_per_seq + t, 0))
    return pl.pallas_call(
        _gla_kernel,
        out_shape=jax.ShapeDtypeStruct((ntok, GLA_DV), BF16),
        grid=(batch, tiles_per_seq),
        in_specs=[tok(GLA_DK), tok(GLA_DK), tok(GLA_DV), tok(GLA_DK), tok(GLA_DV),
                  pl.BlockSpec((1, GLA_HV), lambda b, t: (0, 0))],
        out_specs=tok(GLA_DV),
        scratch_shapes=[pltpu.VMEM((GLA_HEADS, GLA_HK, GLA_HV), F32), pltpu.VMEM((TM, GLA_DV), F32)],
        compiler_params=pltpu.CompilerParams(dimension_semantics=("arbitrary", "arbitrary"),
                                             vmem_limit_bytes=VMEM_LIMIT),
        name="gla",
    )(q, k, v, g, sr, gn)


def _merge_kernel(tiles_per_seq, ya_ref, pooled_ref, ga_ref, gb_ref, h_ref,
                  wa_ref, wp_ref, ps_ref, wb_ref, wo_ref, out_ref):
    i = pl.program_id(0)
    y_a = jnp.dot(ya_ref[...], wa_ref[...], preferred_element_type=F32)
    parts = [jnp.dot(pooled_ref[:, gi * POOL_GDIM:(gi + 1) * POOL_GDIM], wp_ref[gi],
                     preferred_element_type=F32) for gi in range(len(POOL_WINDOWS))]
    yb_in = (jnp.concatenate(parts, axis=1) * ps_ref[...]).astype(BF16)
    y_b = jnp.dot(yb_in, wb_ref[...], preferred_element_type=F32)
    m = (ga_ref[...].astype(F32) * y_a + gb_ref[...].astype(F32) * y_b).astype(BF16)
    h_new = h_ref[...] + jnp.dot(m, wo_ref[...], preferred_element_type=F32)
    row = _row_in_seq(i, tiles_per_seq, h_new.shape)
    out_ref[...] = jnp.where(row >= PAD, h_new, 0.0)


def _merge(ya, pooled, ga, gb, h, wa, wp, ps, wb, wo, tiles_per_seq):
    ntok = h.shape[0]
    tok = pl.BlockSpec((TM, D_MODEL), lambda i: (i, 0))
    sq = _const_spec((D_MODEL, D_MODEL))
    return pl.pallas_call(
        functools.partial(_merge_kernel, tiles_per_seq),
        out_shape=jax.ShapeDtypeStruct((ntok, D_MODEL), F32),
        grid=(ntok // TM,),
        in_specs=[tok, tok, tok, tok, tok, sq,
                  _const_spec((len(POOL_WINDOWS), POOL_GDIM, POOL_GDIM)), _const_spec((1, D_MODEL)), sq, sq],
        out_specs=tok,
        compiler_params=pltpu.CompilerParams(dimension_semantics=("arbitrary",),
                                             vmem_limit_bytes=VMEM_LIMIT),
        name="merge",
    )(ya, pooled, ga, gb, h, wa, wp, ps, wb, wo)


def _ffn_kernel(tiles_per_seq, h_ref, halo_ref, g2_ref, wup_ref, cw_ref, cb_ref, wdn_ref, out_ref):
    i = pl.program_id(0)
    tm = h_ref.shape[0]
    h = h_ref[...]
    hn = _rms(h, g2_ref[...]).astype(BF16)
    hn_halo = _rms(halo_ref[...], g2_ref[...]).astype(BF16)
    hn_ext = jnp.concatenate([hn_halo, hn], axis=0)
    n_chunks = D_FF // FF_CHUNK

    def up_proj(j):
        return jnp.dot(hn_ext, wup_ref[:, 2 * j * FF_CHUNK:2 * (j + 1) * FF_CHUNK], preferred_element_type=F32)

    def activation(up, j):
        cols = slice(2 * j * FF_CHUNK, 2 * (j + 1) * FF_CHUNK)
        c = cb_ref[:, cols]
        for tap in range(CONV_W):
            lo = HALO - (CONV_W - 1) + tap
            c = c + cw_ref[tap:tap + 1, cols] * up[lo:lo + tm, :]
        a, bv = c[:, :FF_CHUNK], c[:, FF_CHUNK:]
        return (a * jax.nn.sigmoid(a) * bv).astype(BF16)

    def down_proj(acts, j_end):
        k1 = j_end * FF_CHUNK
        lhs = acts[0] if len(acts) == 1 else jnp.concatenate(acts, axis=1)
        return jnp.dot(lhs, wdn_ref[k1 - lhs.shape[1]:k1, :], preferred_element_type=F32)

    group_ends = []
    for size in FF_GROUPS:
        group_ends.append((group_ends[-1] if group_ends else 0) + size)
    assert group_ends[-1] == n_chunks

    acc = h
    up = up_proj(0)
    acts, pending = [], None
    for j in range(n_chunks):
        up_next = up_proj(j + 1) if j + 1 < n_chunks else None
        if pending is not None:
            acc = acc + down_proj(*pending)
            pending = None
        acts.append(activation(up, j))
        up = up_next
        if j + 1 in group_ends:
            pending, acts = (acts, j + 1), []
    acc = acc + down_proj(*pending)
    row = _row_in_seq(i, tiles_per_seq, acc.shape)
    out_ref[...] = jnp.where(row >= PAD, acc, 0.0)


def _ffn(h, g2, wup_r, cw_r, cb_r, wdn, tiles_per_seq):
    ntok = h.shape[0]
    tok = pl.BlockSpec((TM, D_MODEL), lambda i: (i, 0))
    halo = pl.BlockSpec((HALO, D_MODEL), lambda i: (jnp.maximum(i * (TM // HALO) - 1, 0), 0))
    return pl.pallas_call(
        functools.partial(_ffn_kernel, tiles_per_seq),
        out_shape=jax.ShapeDtypeStruct((ntok, D_MODEL), F32),
        grid=(ntok // TM,),
        in_specs=[tok, halo, _const_spec((1, D_MODEL)), _const_spec((D_MODEL, 2 * D_FF)),
                  _const_spec((CONV_W, 2 * D_FF)), _const_spec((1, 2 * D_FF)), _const_spec((D_FF, D_MODEL))],
        out_specs=tok,
        compiler_params=pltpu.CompilerParams(dimension_semantics=("arbitrary",),
                                             vmem_limit_bytes=VMEM_LIMIT),
        name="ffn",
    )(h, h, g2, wup_r, cw_r, cb_r, wdn)


def _final_kernel(h_ref, g_ref, out_ref):
    out_ref[...] = _rms(h_ref[...], g_ref[...])


def _final_norm(h, g, batch, seq):
    rows = PAD + N_META
    in_per_seq = (rows + seq) // rows
    out_per_seq = seq // rows
    return pl.pallas_call(
        _final_kernel,
        out_shape=jax.ShapeDtypeStruct((batch * seq, D_MODEL), F32),
        grid=(batch, out_per_seq),
        in_specs=[pl.BlockSpec((rows, D_MODEL), lambda b, t: (b * in_per_seq + t + 1, 0)),
                  pl.BlockSpec((1, D_MODEL), lambda b, t: (0, 0))],
        out_specs=pl.BlockSpec((rows, D_MODEL), lambda b, t: (b * out_per_seq + t, 0)),
        compiler_params=pltpu.CompilerParams(dimension_semantics=("arbitrary", "arbitrary")),
        name="final_norm",
    )(h, g)


def _regroup_ffn(w):
    lead = w.shape[:-1]
    n = D_FF // FF_CHUNK
    w = w.reshape(lead + (2, n, FF_CHUNK))
    w = jnp.swapaxes(w, -3, -2)
    return w.reshape(lead + (2 * D_FF,))


def kernel(x, meta_tokens, norm1_g, w_in, w_gk, b_gk, gla_norm_g, w_a, w_pool_grp, pool_scale,
           w_b, b_gates, w_o, norm2_g, w_up, conv_w, conv_b, w_down, final_norm_g):
    batch, seq, d = x.shape
    depth = w_in.shape[0]
    lp = PAD + N_META + seq
    assert d == D_MODEL and lp % TM == 0 and (PAD + N_META) % CHUNK == 0 and seq % (PAD + N_META) == 0
    tiles_per_seq = lp // TM

    meta = jnp.broadcast_to(meta_tokens.astype(F32)[None], (batch, N_META, d))
    h = jnp.concatenate([jnp.zeros((batch, PAD, d), F32), meta, x.astype(F32)], axis=1)
    h = h.reshape(batch * lp, d)

    glr0 = 2 * GLA_DK + GLA_DV
    for l in range(depth):
        wl = w_in[l]
        w_r = jnp.concatenate(
            [wl[:, :glr0], wl[:, glr0 + GLA_RANK:], wl[:, glr0:glr0 + GLA_RANK],
             jnp.zeros((d, LANES - GLA_RANK), wl.dtype)], axis=1).astype(BF16)
        wgk_pad = jnp.concatenate(
            [w_gk[l], jnp.zeros((LANES - GLA_RANK, GLA_DK), w_gk.dtype)], axis=0).astype(BF16)
        q, k, v, g, sr, pooled, ga, gb = _inproj(
            h, norm1_g[l][None, :], w_r, wgk_pad, b_gk[l][None, :], b_gates[l][None, :], tiles_per_seq)
        ya = _gla(q, k, v, g, sr, gla_norm_g[l][None, :], batch, tiles_per_seq)
        h = _merge(ya, pooled, ga, gb, h, w_a[l].astype(BF16), w_pool_grp[l].astype(BF16),
                   pool_scale[l][None, :], w_b[l].astype(BF16), w_o[l].astype(BF16), tiles_per_seq)
        h = _ffn(h, norm2_g[l][None, :], _regroup_ffn(w_up[l].astype(BF16)), _regroup_ffn(conv_w[l]),
                 _regroup_ffn(conv_b[l])[None, :], w_down[l].astype(BF16), tiles_per_seq)

    out = _final_norm(h, final_norm_g[None, :], batch, seq)
    return out.reshape(batch, seq, d)
```

```python
import functools

import jax
import jax.numpy as jnp
from jax import lax
from jax.experimental import pallas as pl
from jax.experimental.pallas import tpu as pltpu

F32 = jnp.float32
BF16 = jnp.bfloat16

D_MODEL = 1024
N_META = 16
GLA_HEADS = 4
GLA_HK = 128
GLA_HV = 256
GLA_DK = GLA_HEADS * GLA_HK
GLA_DV = GLA_HEADS * GLA_HV
GLA_RANK = 16
GLA_TAU = 16.0
CHUNK = 64
POOL_WINDOWS = (2, 4, 8, 16)
POOL_GDIM = 256
D_FF = 2816
CONV_W = 3
EPS = 1e-6

PAD = 240
HALO = 16
TM = 768
LEAD = PAD + N_META
PARTS = TM // LEAD
INPROJ_COLS = 512
FF_CHUNK = 256
FF_GROUPS = (4, 4, 2, 1)
LANES = 128
VMEM_LIMIT = 60 * 1024 * 1024

_Q0, _K0, _V0, _R0, _U0, _GA0, _GB0, _GLR0 = 0, 512, 1024, 2048, 3072, 4096, 5120, 6144
IN_COLS = _GLR0 + LANES


def _rms(x, g):
    ms = jnp.mean(x * x, axis=-1, keepdims=True)
    return x * lax.rsqrt(ms + EPS) * g


def _row_in_seq(tile_idx, tiles_per_seq, shape):
    base = (tile_idx % tiles_per_seq) * shape[0]
    return base + lax.broadcasted_iota(jnp.int32, shape, 0)


def _const_spec(shape):
    nd = len(shape)
    return pl.BlockSpec(shape, lambda *_: (0,) * nd, pipeline_mode=pl.Buffered(1))


def _resid_specs(first_layer, tiles_per_seq, seq):
    def spec(p):
        if first_layer:
            def imap(i):
                b, t = i // tiles_per_seq, i % tiles_per_seq
                return (jnp.maximum(b * (seq // LEAD) + PARTS * t - 1 + p, 0), 0)
        else:
            imap = lambda i: (PARTS * i + p, 0)
        return pl.BlockSpec((LEAD, D_MODEL), imap)
    return [spec(p) for p in range(PARTS)]


def _load_resid(part_refs, head_ref, first_layer, tile_in_seq):
    parts = [r[...] for r in part_refs]
    if first_layer:
        parts[0] = jnp.where(tile_in_seq == 0, head_ref[...], parts[0])
    return jnp.concatenate(parts, axis=0)


def _inproj_kernel(tiles_per_seq, first_layer, h0_ref, h1_ref, h2_ref, head_ref, g1_ref, w_ref, wgk_ref, bgk_ref,
                   bg_ref, q_ref, k_ref, v_ref, g_ref, sr_ref, pooled_ref, ga_ref, gb_ref, carry_ref):
    i = pl.program_id(0)
    tm = TM

    @pl.when(i == 0)
    def _():
        carry_ref[...] = jnp.zeros_like(carry_ref)

    h = _load_resid((h0_ref, h1_ref, h2_ref), head_ref, first_layer, i % tiles_per_seq)
    hn = _rms(h, g1_ref[...]).astype(BF16)

    half = INPROJ_COLS
    head_pos = (i % tiles_per_seq) * tm + lax.broadcasted_iota(jnp.int32, (HALO, POOL_GDIM), 0)
    stash = {}

    def proj(c0, c1):
        return lambda: jnp.dot(hn, w_ref[:, c0:c1], preferred_element_type=F32)

    def pool_epilogue(first_group):
        def fn(u):
            c0 = first_group * POOL_GDIM
            ext = jnp.concatenate([carry_ref[:, c0:c0 + half], u], axis=0)
            carry_ref[:, c0:c0 + half] = u[tm - HALO:, :]
            for gl in range(half // POOL_GDIM):
                w = POOL_WINDOWS[first_group + gl]
                loc = slice(gl * POOL_GDIM, (gl + 1) * POOL_GDIM)
                s = ext[:, loc]
                sh = 1
                while sh < w:
                    s = s + pltpu.roll(s, sh, axis=0)
                    sh *= 2
                out_cols = slice(c0 + gl * POOL_GDIM, c0 + (gl + 1) * POOL_GDIM)
                mean = s[HALO:, :] * (1.0 / w)
                pooled_ref[:, out_cols] = (mean - u[:, loc]).astype(BF16)
                cnt = jnp.clip(head_pos + 1, 1, w).astype(F32)
                pooled_ref[PAD:PAD + HALO, out_cols] = (
                    s[HALO + PAD:2 * HALO + PAD, :] / cnt - u[PAD:PAD + HALO, loc]).astype(BF16)
        return fn

    def glr_epilogue(x):
        stash["glr"] = x.astype(BF16)

    def forget_gate_epilogue(z):
        z = z + bgk_ref[...]
        g_ref[...] = (jnp.minimum(z, 0.0) - jnp.log(1.0 + jnp.exp(-jnp.abs(z)))) * (1.0 / GLA_TAU)

    def gate_epilogue(out_ref, bias0, c0):
        def fn(x):
            out_ref[:, c0:c0 + half] = jax.nn.sigmoid(x + bg_ref[:, bias0 + c0:bias0 + c0 + half]).astype(BF16)
        return fn

    def silu_epilogue(c0):
        def fn(x):
            sr_ref[:, c0:c0 + half] = (x * jax.nn.sigmoid(x)).astype(BF16)
        return fn

    def cast_epilogue(out_ref, c0, scale=None):
        def fn(x):
            out_ref[:, c0:c0 + half] = (x if scale is None else x * scale).astype(BF16)
        return fn

    assert 2 * half == D_MODEL == GLA_DV and half == GLA_DK
    gate = lambda ref, col0, bias0, c0: (proj(col0 + c0, col0 + c0 + half), gate_epilogue(ref, bias0, c0))
    stages = [
        (proj(_GLR0, IN_COLS), glr_epilogue),
        (proj(_U0 + half, _GA0), pool_epilogue(half // POOL_GDIM)),
        gate(ga_ref, _GA0, 0, 0),
        (lambda: jnp.dot(stash["glr"], wgk_ref[...], preferred_element_type=F32), forget_gate_epilogue),
        gate(ga_ref, _GA0, 0, half),
        (proj(_U0, _U0 + half), pool_epilogue(0)),
        gate(gb_ref, _GB0, D_MODEL, 0),
        (proj(_R0, _R0 + half), silu_epilogue(0)),
        gate(gb_ref, _GB0, D_MODEL, half),
        (proj(_R0 + half, _U0), silu_epilogue(half)),
        (proj(_V0, _V0 + half), cast_epilogue(v_ref, 0)),
        (proj(_V0 + half, _R0), cast_epilogue(v_ref, half)),
        (proj(_K0, _V0), cast_epilogue(k_ref, 0)),
        (proj(_Q0, _K0), cast_epilogue(q_ref, 0, GLA_HK ** -0.5)),
    ]

    res = stages[0][0]()
    for idx, (_, epilogue) in enumerate(stages):
        nxt = stages[idx + 1][0]() if idx + 1 < len(stages) else None
        epilogue(res)
        res = nxt


def _inproj(h, head, g1, w_r, wgk_pad, bgk, bgates, ntok, tiles_per_seq, first_layer, seq):
    tok = lambda n: pl.BlockSpec((TM, n), lambda i: (i, 0))
    out_shape = (
        jax.ShapeDtypeStruct((ntok, GLA_DK), BF16),
        jax.ShapeDtypeStruct((ntok, GLA_DK), BF16),
        jax.ShapeDtypeStruct((ntok, GLA_DV), BF16),
        jax.ShapeDtypeStruct((ntok, GLA_DK), F32),
        jax.ShapeDtypeStruct((ntok, GLA_DV), BF16),
        jax.ShapeDtypeStruct((ntok, D_MODEL), BF16),
        jax.ShapeDtypeStruct((ntok, D_MODEL), BF16),
        jax.ShapeDtypeStruct((ntok, D_MODEL), BF16),
    )
    return pl.pallas_call(
        functools.partial(_inproj_kernel, tiles_per_seq, first_layer),
        out_shape=out_shape,
        grid=(ntok // TM,),
        in_specs=_resid_specs(first_layer, tiles_per_seq, seq) + [
            _const_spec((LEAD, D_MODEL)), _const_spec((1, D_MODEL)), _const_spec((D_MODEL, IN_COLS)),
            _const_spec((LANES, GLA_DK)), _const_spec((1, GLA_DK)), _const_spec((1, 2 * D_MODEL))],
        out_specs=(tok(GLA_DK), tok(GLA_DK), tok(GLA_DV), tok(GLA_DK), tok(GLA_DV),
                   tok(D_MODEL), tok(D_MODEL), tok(D_MODEL)),
        scratch_shapes=[pltpu.VMEM((HALO, D_MODEL), F32)],
        compiler_params=pltpu.CompilerParams(dimension_semantics=("arbitrary",),
                                             vmem_limit_bytes=VMEM_LIMIT),
        name="inproj",
    )(h, h, h, head, g1, w_r, wgk_pad, bgk, bgates)


def _gla_kernel(q_ref, k_ref, v_ref, g_ref, sr_ref, gn_ref, y_ref, state_ref, o_ref):
    t = pl.program_id(1)
    tg = q_ref.shape[0]

    @pl.when(t == 0)
    def _():
        state_ref[...] = jnp.zeros_like(state_ref)

    rr = lax.broadcasted_iota(jnp.int32, (CHUNK, CHUNK), 0)
    cc = lax.broadcasted_iota(jnp.int32, (CHUNK, CHUNK), 1)
    causal = rr >= cc
    tril = causal.astype(BF16)

    n_chunks = tg // CHUNK
    units = [(c, h) for c in range(n_chunks) for h in range(GLA_HEADS)]
    rows = lambda c: slice(c * CHUNK, (c + 1) * CHUNK)
    kcols = lambda h: slice(h * GLA_HK, (h + 1) * GLA_HK)
    vcols = lambda h: slice(h * GLA_HV, (h + 1) * GLA_HV)

    g = g_ref[...]
    g_hi = g.astype(BF16)
    g_lo = (g - g_hi.astype(F32)).astype(BF16)
    b_chunks = [jnp.dot(tril, g_hi[rows(c)], preferred_element_type=F32)
                + jnp.dot(tril, g_lo[rows(c)], preferred_element_type=F32) for c in range(n_chunks)]
    b = jnp.concatenate(b_chunks, axis=0)
    b_last = [bc[CHUNK - 1:CHUNK, :] for bc in b_chunks]
    b_end = jnp.concatenate([jnp.broadcast_to(bl, (CHUNK, GLA_DK)) for bl in b_last], axis=0)
    q = q_ref[...].astype(F32)
    k = k_ref[...].astype(F32)
    q_dec = (q * jnp.exp(b)).astype(BF16)
    k_inv = (k * jnp.exp(-b)).astype(BF16)
    k_end = (k * jnp.exp(b_end - b)).astype(BF16)
    dec = [jnp.exp(bl) for bl in b_last]

    att = {}
    for c, h in units:
        a = lax.dot_general(q_dec[rows(c), kcols(h)], k_inv[rows(c), kcols(h)], (((1,), (1,)), ((), ())),
                            preferred_element_type=F32)
        att[c, h] = jnp.where(causal, a, 0.0).astype(BF16)
    for c, h in units:
        o_ref[rows(c), vcols(h)] = jnp.dot(att[c, h], v_ref[rows(c), vcols(h)], preferred_element_type=F32)
    d_state = {}
    for c, h in units:
        d_state[c, h] = lax.dot_general(k_end[rows(c), kcols(h)], v_ref[rows(c), vcols(h)],
                                        (((0,), (0,)), ((), ())), preferred_element_type=F32)

    state = [state_ref[h] for h in range(GLA_HEADS)]
    for c, h in units:
        o_ref[rows(c), vcols(h)] += jnp.dot(q_dec[rows(c), kcols(h)], state[h].astype(BF16),
                                            preferred_element_type=F32)
        dec_col = jnp.broadcast_to(dec[c][:, kcols(h)], (GLA_HK, GLA_HK)).T
        state[h] = jnp.concatenate([dec_col, dec_col], axis=1) * state[h] + d_state[c, h]
    for h in range(GLA_HEADS):
        state_ref[h] = state[h]

    for h in range(GLA_HEADS):
        vs = slice(h * GLA_HV, (h + 1) * GLA_HV)
        y_ref[:, vs] = (_rms(o_ref[:, vs], gn_ref[...]) * sr_ref[:, vs].astype(F32)).astype(BF16)


def _gla(q, k, v, g, sr, gn, batch, tiles_per_seq):
    ntok = q.shape[0]
    tok = lambda n: pl.BlockSpec((TM, n), lambda b, t: (b * tiles_per_seq + t, 0))
    return pl.pallas_call(
        _gla_kernel,
        out_shape=jax.ShapeDtypeStruct((ntok, GLA_DV), BF16),
        grid=(batch, tiles_per_seq),
        in_specs=[tok(GLA_DK), tok(GLA_DK), tok(GLA_DV), tok(GLA_DK), tok(GLA_DV),
                  pl.BlockSpec((1, GLA_HV), lambda b, t: (0, 0))],
        out_specs=tok(GLA_DV),
        scratch_shapes=[pltpu.VMEM((GLA_HEADS, GLA_HK, GLA_HV), F32), pltpu.VMEM((TM, GLA_DV), F32)],
        compiler_params=pltpu.CompilerParams(dimension_semantics=("arbitrary", "arbitrary"),
                                             vmem_limit_bytes=VMEM_LIMIT),
        name="gla",
    )(q, k, v, g, sr, gn)


def _merge_kernel(tiles_per_seq, first_layer, ya_ref, pooled_ref, ga_ref, gb_ref, h0_ref, h1_ref, h2_ref, head_ref,
                  wa_ref, wp_ref, ps_ref, wb_ref, wo_ref, out_ref):
    i = pl.program_id(0)
    h = _load_resid((h0_ref, h1_ref, h2_ref), head_ref, first_layer, i % tiles_per_seq)
    y_a = jnp.dot(ya_ref[...], wa_ref[...], preferred_element_type=F32)
    parts = [jnp.dot(pooled_ref[:, gi * POOL_GDIM:(gi + 1) * POOL_GDIM], wp_ref[gi],
                     preferred_element_type=F32) for gi in range(len(POOL_WINDOWS))]
    yb_in = (jnp.concatenate(parts, axis=1) * ps_ref[...]).astype(BF16)
    y_b = jnp.dot(yb_in, wb_ref[...], preferred_element_type=F32)
    m = (ga_ref[...].astype(F32) * y_a + gb_ref[...].astype(F32) * y_b).astype(BF16)
    h_new = h + jnp.dot(m, wo_ref[...], preferred_element_type=F32)
    row = _row_in_seq(i, tiles_per_seq, h_new.shape)
    out_ref[...] = jnp.where(row >= PAD, h_new, 0.0)


def _merge(ya, pooled, ga, gb, h, head, wa, wp, ps, wb, wo, tiles_per_seq, first_layer, seq):
    ntok = ya.shape[0]
    tok = pl.BlockSpec((TM, D_MODEL), lambda i: (i, 0))
    sq = _const_spec((D_MODEL, D_MODEL))
    return pl.pallas_call(
        functools.partial(_merge_kernel, tiles_per_seq, first_layer),
        out_shape=jax.ShapeDtypeStruct((ntok, D_MODEL), F32),
        grid=(ntok // TM,),
        in_specs=[tok, tok, tok, tok] + _resid_specs(first_layer, tiles_per_seq, seq) + [
            _const_spec((LEAD, D_MODEL)), sq,
            _const_spec((len(POOL_WINDOWS), POOL_GDIM, POOL_GDIM)), _const_spec((1, D_MODEL)), sq, sq],
        out_specs=tok,
        compiler_params=pltpu.CompilerParams(dimension_semantics=("arbitrary",),
                                             vmem_limit_bytes=VMEM_LIMIT),
        name="merge",
    )(ya, pooled, ga, gb, h, h, h, head, wa, wp, ps, wb, wo)


def _ffn_kernel(tiles_per_seq, seq, last_layer, h_ref, halo_ref, g2_ref, wup_ref, cw_ref, cb_ref, wdn_ref, gf_ref,
                out_ref, *scratch):
    i = pl.program_id(0)
    tm = h_ref.shape[0]
    h = h_ref[...]
    hn = _rms(h, g2_ref[...]).astype(BF16)
    hn_halo = _rms(halo_ref[...], g2_ref[...]).astype(BF16)
    hn_ext = jnp.concatenate([hn_halo, hn], axis=0)
    n_chunks = D_FF // FF_CHUNK

    def pair(ref, j, rows=slice(None)):
        return jnp.concatenate([ref[rows, j * FF_CHUNK:(j + 1) * FF_CHUNK],
                                ref[rows, D_FF + j * FF_CHUNK:D_FF + (j + 1) * FF_CHUNK]], axis=1)

    def up_proj(j):
        return jnp.dot(hn_ext, pair(wup_ref, j), preferred_element_type=F32)

    def activation(up, j):
        c = pair(cb_ref, j)
        for tap in range(CONV_W):
            lo = HALO - (CONV_W - 1) + tap
            c = c + pair(cw_ref, j, slice(tap, tap + 1)) * up[lo:lo + tm, :]
        a, bv = c[:, :FF_CHUNK], c[:, FF_CHUNK:]
        return (a * jax.nn.sigmoid(a) * bv).astype(BF16)

    def down_proj(acts, j_end):
        k1 = j_end * FF_CHUNK
        lhs = acts[0] if len(acts) == 1 else jnp.concatenate(acts, axis=1)
        return jnp.dot(lhs, wdn_ref[k1 - lhs.shape[1]:k1, :], preferred_element_type=F32)

    group_ends = []
    for size in FF_GROUPS:
        group_ends.append((group_ends[-1] if group_ends else 0) + size)
    assert group_ends[-1] == n_chunks

    acc = h
    up = up_proj(0)
    acts, pending = [], None
    for j in range(n_chunks):
        up_next = up_proj(j + 1) if j + 1 < n_chunks else None
        if pending is not None:
            acc = acc + down_proj(*pending)
            pending = None
        acts.append(activation(up, j))
        up = up_next
        if j + 1 in group_ends:
            pending, acts = (acts, j + 1), []
    acc = acc + down_proj(*pending)

    if not last_layer:
        row = _row_in_seq(i, tiles_per_seq, acc.shape)
        out_ref[...] = jnp.where(row >= PAD, acc, 0.0)
        return

    obuf_ref, sem_ref = scratch
    n_steps = pl.num_programs(0)

    def tile_copy(step, head_tile):
        slot, b, t = step % 2, step // tiles_per_seq, step % tiles_per_seq
        if head_tile:
            src = obuf_ref.at[slot, pl.ds(LEAD, tm - LEAD)]
            dst = out_ref.at[pl.ds(b * seq, tm - LEAD)]
        else:
            src = obuf_ref.at[slot]
            dst = out_ref.at[pl.ds(b * seq + t * tm - LEAD, tm)]
        return pltpu.make_async_copy(src, dst, sem_ref.at[slot])

    def for_tile(step, action):
        head_tile = step % tiles_per_seq == 0

        @pl.when(head_tile)
        def _():
            action(tile_copy(step, True))

        @pl.when(jnp.logical_not(head_tile))
        def _():
            action(tile_copy(step, False))

    @pl.when(i >= 2)
    def _():
        for_tile(i - 2, lambda cp: cp.wait())

    obuf_ref[i % 2] = _rms(acc, gf_ref[...])
    for_tile(i, lambda cp: cp.start())

    @pl.when(i == n_steps - 1)
    def _():
        @pl.when(i >= 1)
        def _():
            for_tile(i - 1, lambda cp: cp.wait())
        for_tile(i, lambda cp: cp.wait())


def _ffn(h, g2, wup, cw, cb, wdn, gf, tiles_per_seq, seq, last_layer):
    ntok = h.shape[0]
    tok = pl.BlockSpec((TM, D_MODEL), lambda i: (i, 0))
    halo = pl.BlockSpec((HALO, D_MODEL), lambda i: (jnp.maximum(i * (TM // HALO) - 1, 0), 0))
    if last_layer:
        out_rows = ntok // (LEAD + seq) * seq
        out_spec = pl.BlockSpec(memory_space=pl.ANY)
        scratch = [pltpu.VMEM((2, TM, D_MODEL), F32), pltpu.SemaphoreType.DMA((2,))]
    else:
        out_rows, out_spec, scratch = ntok, tok, []
    return pl.pallas_call(
        functools.partial(_ffn_kernel, tiles_per_seq, seq, last_layer),
        out_shape=jax.ShapeDtypeStruct((out_rows, D_MODEL), F32),
        grid=(ntok // TM,),
        in_specs=[tok, halo, _const_spec((1, D_MODEL)), _const_spec((D_MODEL, 2 * D_FF)),
                  _const_spec((CONV_W, 2 * D_FF)), _const_spec((1, 2 * D_FF)), _const_spec((D_FF, D_MODEL)),
                  _const_spec((1, D_MODEL))],
        out_specs=out_spec,
        scratch_shapes=scratch,
        compiler_params=pltpu.CompilerParams(dimension_semantics=("arbitrary",),
                                             vmem_limit_bytes=VMEM_LIMIT),
        name="ffn",
    )(h, h, g2, wup, cw, cb, wdn, gf)


def kernel(x, meta_tokens, norm1_g, w_in, w_gk, b_gk, gla_norm_g, w_a, w_pool_grp, pool_scale,
           w_b, b_gates, w_o, norm2_g, w_up, conv_w, conv_b, w_down, final_norm_g):
    batch, seq, d = x.shape
    depth = w_in.shape[0]
    lp = LEAD + seq
    assert d == D_MODEL and lp % TM == 0 and TM % LEAD == 0 and LEAD % CHUNK == 0 and seq % LEAD == 0
    tiles_per_seq = lp // TM
    ntok = batch * lp

    head = jnp.concatenate([jnp.zeros((PAD, d), F32), meta_tokens.astype(F32)], axis=0)
    h = x.astype(F32).reshape(batch * seq, d)

    glr0 = 2 * GLA_DK + GLA_DV
    for l in range(depth):
        first, last = l == 0, l == depth - 1
        wl = w_in[l]
        w_r = jnp.concatenate(
            [wl[:, :glr0], wl[:, glr0 + GLA_RANK:], wl[:, glr0:glr0 + GLA_RANK],
             jnp.zeros((d, LANES - GLA_RANK), wl.dtype)], axis=1).astype(BF16)
        wgk_pad = jnp.concatenate(
            [w_gk[l], jnp.zeros((LANES - GLA_RANK, GLA_DK), w_gk.dtype)], axis=0).astype(BF16)
        q, k, v, g, sr, pooled, ga, gb = _inproj(
            h, head, norm1_g[l][None, :], w_r, wgk_pad, b_gk[l][None, :], b_gates[l][None, :],
            ntok, tiles_per_seq, first, seq)
        ya = _gla(q, k, v, g, sr, gla_norm_g[l][None, :], batch, tiles_per_seq)
        h = _merge(ya, pooled, ga, gb, h, head, w_a[l].astype(BF16), w_pool_grp[l].astype(BF16),
                   pool_scale[l][None, :], w_b[l].astype(BF16), w_o[l].astype(BF16), tiles_per_seq, first, seq)
        h = _ffn(h, norm2_g[l][None, :], w_up[l].astype(BF16), conv_w[l], conv_b[l][None, :],
                 w_down[l].astype(BF16), final_norm_g[None, :], tiles_per_seq, seq, last)

    return h.reshape(batch, seq, d)
```

```python
import functools

import jax
import jax.numpy as jnp
from jax import lax
from jax.experimental import pallas as pl
from jax.experimental.pallas import tpu as pltpu

F32 = jnp.float32
BF16 = jnp.bfloat16

D_MODEL = 1024
N_META = 16
GLA_HEADS = 4
GLA_HK = 128
GLA_HV = 256
GLA_DK = GLA_HEADS * GLA_HK
GLA_DV = GLA_HEADS * GLA_HV
GLA_RANK = 16
GLA_TAU = 16.0
CHUNK = 64
POOL_WINDOWS = (2, 4, 8, 16)
POOL_GDIM = 256
D_FF = 2816
CONV_W = 3
EPS = 1e-6

PAD = 240
HALO = 16
TM = 768
LEAD = PAD + N_META
PARTS = TM // LEAD
INPROJ_COLS = 512
FF_CHUNK = 256
FF_GROUPS = (4, 4, 2, 1)
LANES = 128
SUBLANES = 8
VMEM_LIMIT = 60 * 1024 * 1024

_Q0, _K0, _V0, _R0, _U0, _GA0, _GB0, _GLR0 = 0, 512, 1024, 2048, 3072, 4096, 5120, 6144
IN_COLS = _GLR0 + LANES


def _rms(x, g):
    ms = jnp.mean(x * x, axis=-1, keepdims=True)
    return x * lax.rsqrt(ms + EPS) * g


def _row_in_seq(tile_idx, tiles_per_seq, shape):
    base = (tile_idx % tiles_per_seq) * shape[0]
    return base + lax.broadcasted_iota(jnp.int32, shape, 0)


def _const_spec(shape):
    nd = len(shape)
    return pl.BlockSpec(shape, lambda *_: (0,) * nd, pipeline_mode=pl.Buffered(1))


def _layer_spec(shape, layer):
    nd = len(shape)
    return pl.BlockSpec((None,) + tuple(shape), lambda *_: (layer,) + (0,) * nd, pipeline_mode=pl.Buffered(1))


def _resid_specs(first_layer, tiles_per_seq, seq):
    def spec(p):
        if first_layer:
            def imap(i):
                b, t = i // tiles_per_seq, i % tiles_per_seq
                return (jnp.maximum(b * (seq // LEAD) + PARTS * t - 1 + p, 0), 0)
        else:
            imap = lambda i: (PARTS * i + p, 0)
        return pl.BlockSpec((LEAD, D_MODEL), imap)
    return [spec(p) for p in range(PARTS)]


def _load_resid(part_refs, head_ref, first_layer, tile_in_seq):
    parts = [r[...] for r in part_refs]
    if first_layer:
        parts[0] = jnp.where(tile_in_seq == 0, head_ref[...], parts[0])
    return jnp.concatenate(parts, axis=0)


def _inproj_kernel(tiles_per_seq, first_layer, h0_ref, h1_ref, h2_ref, head_ref, g1_ref, w_ref, wgk_ref, bgk_ref,
                   bg_ref, q_ref, k_ref, v_ref, g_ref, sr_ref, pooled_ref, ga_ref, gb_ref, carry_ref):
    i = pl.program_id(0)
    tm = TM

    @pl.when(i == 0)
    def _():
        carry_ref[...] = jnp.zeros_like(carry_ref)

    h = _load_resid((h0_ref, h1_ref, h2_ref), head_ref, first_layer, i % tiles_per_seq)
    hn = _rms(h, g1_ref[...]).astype(BF16)

    half = INPROJ_COLS
    head_pos = (i % tiles_per_seq) * tm + lax.broadcasted_iota(jnp.int32, (HALO, POOL_GDIM), 0)
    stash = {}

    def proj(c0, c1):
        return lambda: jnp.dot(hn, w_ref[:, c0:c1], preferred_element_type=F32)

    def pool_epilogue(first_group):
        def fn(u):
            c0 = first_group * POOL_GDIM
            ext = jnp.concatenate([carry_ref[:, c0:c0 + half], u], axis=0)
            carry_ref[:, c0:c0 + half] = u[tm - HALO:, :]
            for gl in range(half // POOL_GDIM):
                w = POOL_WINDOWS[first_group + gl]
                loc = slice(gl * POOL_GDIM, (gl + 1) * POOL_GDIM)
                s = ext[:, loc]
                sh = 1
                while sh < w:
                    s = s + pltpu.roll(s, sh, axis=0)
                    sh *= 2
                out_cols = slice(c0 + gl * POOL_GDIM, c0 + (gl + 1) * POOL_GDIM)
                mean = s[HALO:, :] * (1.0 / w)
                pooled_ref[:, out_cols] = (mean - u[:, loc]).astype(BF16)
                cnt = jnp.clip(head_pos + 1, 1, w).astype(F32)
                pooled_ref[PAD:PAD + HALO, out_cols] = (
                    s[HALO + PAD:2 * HALO + PAD, :] / cnt - u[PAD:PAD + HALO, loc]).astype(BF16)
        return fn

    def glr_epilogue(x):
        stash["glr"] = x.astype(BF16)

    def forget_gate_epilogue(z):
        z = z + bgk_ref[...]
        g_ref[...] = (jnp.minimum(z, 0.0) - jnp.log(1.0 + jnp.exp(-jnp.abs(z)))) * (1.0 / GLA_TAU)

    def gate_epilogue(out_ref, bias0, c0):
        def fn(x):
            out_ref[:, c0:c0 + half] = jax.nn.sigmoid(x + bg_ref[:, bias0 + c0:bias0 + c0 + half]).astype(BF16)
        return fn

    def silu_epilogue(c0):
        def fn(x):
            sr_ref[:, c0:c0 + half] = (x * jax.nn.sigmoid(x)).astype(BF16)
        return fn

    def cast_epilogue(out_ref, c0, scale=None):
        def fn(x):
            out_ref[:, c0:c0 + half] = (x if scale is None else x * scale).astype(BF16)
        return fn

    assert 2 * half == D_MODEL == GLA_DV and half == GLA_DK
    gate = lambda ref, col0, bias0, c0: (proj(col0 + c0, col0 + c0 + half), gate_epilogue(ref, bias0, c0))
    stages = [
        (proj(_GLR0, IN_COLS), glr_epilogue),
        (proj(_U0 + half, _GA0), pool_epilogue(half // POOL_GDIM)),
        gate(ga_ref, _GA0, 0, 0),
        (lambda: jnp.dot(stash["glr"], wgk_ref[...], preferred_element_type=F32), forget_gate_epilogue),
        gate(ga_ref, _GA0, 0, half),
        (proj(_U0, _U0 + half), pool_epilogue(0)),
        gate(gb_ref, _GB0, D_MODEL, 0),
        (proj(_R0, _R0 + half), silu_epilogue(0)),
        gate(gb_ref, _GB0, D_MODEL, half),
        (proj(_R0 + half, _U0), silu_epilogue(half)),
        (proj(_V0, _V0 + half), cast_epilogue(v_ref, 0)),
        (proj(_V0 + half, _R0), cast_epilogue(v_ref, half)),
        (proj(_K0, _V0), cast_epilogue(k_ref, 0)),
        (proj(_Q0, _K0), cast_epilogue(q_ref, 0, GLA_HK ** -0.5)),
    ]

    res = stages[0][0]()
    for idx, (_, epilogue) in enumerate(stages):
        nxt = stages[idx + 1][0]() if idx + 1 < len(stages) else None
        epilogue(res)
        res = nxt


def _inproj(h, head, g1, w_r, wgk_pad, bgk, bgates, ntok, tiles_per_seq, layer, seq):
    first_layer = layer == 0
    tok = lambda n: pl.BlockSpec((TM, n), lambda i: (i, 0))
    out_shape = (
        jax.ShapeDtypeStruct((ntok, GLA_DK), BF16),
        jax.ShapeDtypeStruct((ntok, GLA_DK), BF16),
        jax.ShapeDtypeStruct((ntok, GLA_DV), BF16),
        jax.ShapeDtypeStruct((ntok, GLA_DK), F32),
        jax.ShapeDtypeStruct((ntok, GLA_DV), BF16),
        jax.ShapeDtypeStruct((ntok, D_MODEL), BF16),
        jax.ShapeDtypeStruct((ntok, D_MODEL), BF16),
        jax.ShapeDtypeStruct((ntok, D_MODEL), BF16),
    )
    return pl.pallas_call(
        functools.partial(_inproj_kernel, tiles_per_seq, first_layer),
        out_shape=out_shape,
        grid=(ntok // TM,),
        in_specs=_resid_specs(first_layer, tiles_per_seq, seq) + [
            _const_spec((LEAD, D_MODEL)), _const_spec((1, D_MODEL)), _layer_spec((D_MODEL, IN_COLS), layer),
            _layer_spec((LANES, GLA_DK), layer), _const_spec((1, GLA_DK)), _const_spec((1, 2 * D_MODEL))],
        out_specs=(tok(GLA_DK), tok(GLA_DK), tok(GLA_DV), tok(GLA_DK), tok(GLA_DV),
                   tok(D_MODEL), tok(D_MODEL), tok(D_MODEL)),
        scratch_shapes=[pltpu.VMEM((HALO, D_MODEL), F32)],
        compiler_params=pltpu.CompilerParams(dimension_semantics=("arbitrary",),
                                             vmem_limit_bytes=VMEM_LIMIT),
        name="inproj",
    )(h, h, h, head, g1, w_r, wgk_pad, bgk, bgates)


def _gla_kernel(q_ref, k_ref, v_ref, g_ref, sr_ref, gn_ref, y_ref, state_ref, o_ref):
    t = pl.program_id(1)
    tg = q_ref.shape[0]

    @pl.when(t == 0)
    def _():
        state_ref[...] = jnp.zeros_like(state_ref)

    rr = lax.broadcasted_iota(jnp.int32, (CHUNK, CHUNK), 0)
    cc = lax.broadcasted_iota(jnp.int32, (CHUNK, CHUNK), 1)
    causal = rr >= cc
    tril = causal.astype(BF16)

    n_chunks = tg // CHUNK
    units = [(c, h) for c in range(n_chunks) for h in range(GLA_HEADS)]
    rows = lambda c: slice(c * CHUNK, (c + 1) * CHUNK)
    kcols = lambda h: slice(h * GLA_HK, (h + 1) * GLA_HK)
    vcols = lambda h: slice(h * GLA_HV, (h + 1) * GLA_HV)

    g = g_ref[...]
    g_hi = g.astype(BF16)
    g_lo = (g - g_hi.astype(F32)).astype(BF16)
    b_chunks = [jnp.dot(tril, g_hi[rows(c)], preferred_element_type=F32)
                + jnp.dot(tril, g_lo[rows(c)], preferred_element_type=F32) for c in range(n_chunks)]
    b = jnp.concatenate(b_chunks, axis=0)
    b_last = [bc[CHUNK - 1:CHUNK, :] for bc in b_chunks]
    b_end = jnp.concatenate([jnp.broadcast_to(bl, (CHUNK, GLA_DK)) for bl in b_last], axis=0)
    q = q_ref[...].astype(F32)
    k = k_ref[...].astype(F32)
    q_dec = (q * jnp.exp(b)).astype(BF16)
    k_inv = (k * jnp.exp(-b)).astype(BF16)
    k_end = (k * jnp.exp(b_end - b)).astype(BF16)
    dec = [jnp.exp(bl) for bl in b_last]

    att = {}
    for c, h in units:
        a = lax.dot_general(q_dec[rows(c), kcols(h)], k_inv[rows(c), kcols(h)], (((1,), (1,)), ((), ())),
                            preferred_element_type=F32)
        att[c, h] = jnp.where(causal, a, 0.0).astype(BF16)
    for c, h in units:
        o_ref[rows(c), vcols(h)] = jnp.dot(att[c, h], v_ref[rows(c), vcols(h)], preferred_element_type=F32)
    d_state = {}
    for c, h in units:
        d_state[c, h] = lax.dot_general(k_end[rows(c), kcols(h)], v_ref[rows(c), vcols(h)],
                                        (((0,), (0,)), ((), ())), preferred_element_type=F32)

    state = [state_ref[h] for h in range(GLA_HEADS)]
    for c, h in units:
        o_ref[rows(c), vcols(h)] += jnp.dot(q_dec[rows(c), kcols(h)], state[h].astype(BF16),
                                            preferred_element_type=F32)
        dec_col = jnp.broadcast_to(dec[c][:, kcols(h)], (GLA_HK, GLA_HK)).T
        state[h] = jnp.concatenate([dec_col, dec_col], axis=1) * state[h] + d_state[c, h]
    for h in range(GLA_HEADS):
        state_ref[h] = state[h]

    for h in range(GLA_HEADS):
        vs = slice(h * GLA_HV, (h + 1) * GLA_HV)
        y_ref[:, vs] = (_rms(o_ref[:, vs], gn_ref[...]) * sr_ref[:, vs].astype(F32)).astype(BF16)


def _gla(q, k, v, g, sr, gn, batch, tiles_per_seq):
    ntok = q.shape[0]
    tok = lambda n: pl.BlockSpec((TM, n), lambda b, t: (b * tiles_per_seq + t, 0))
    return pl.pallas_call(
        _gla_kernel,
        out_shape=jax.ShapeDtypeStruct((ntok, GLA_DV), BF16),
        grid=(batch, tiles_per_seq),
        in_specs=[tok(GLA_DK), tok(GLA_DK), tok(GLA_DV), tok(GLA_DK), tok(GLA_DV), _const_spec((1, GLA_HV))],
        out_specs=tok(GLA_DV),
        scratch_shapes=[pltpu.VMEM((GLA_HEADS, GLA_HK, GLA_HV), F32), pltpu.VMEM((TM, GLA_DV), F32)],
        compiler_params=pltpu.CompilerParams(dimension_semantics=("arbitrary", "arbitrary"),
                                             vmem_limit_bytes=VMEM_LIMIT),
        name="gla",
    )(q, k, v, g, sr, gn)


def _merge_kernel(tiles_per_seq, first_layer, ya_ref, pooled_ref, ga_ref, gb_ref, h0_ref, h1_ref, h2_ref, head_ref,
                  wa_ref, wp_ref, ps_ref, wb_ref, wo_ref, out_ref):
    i = pl.program_id(0)
    h = _load_resid((h0_ref, h1_ref, h2_ref), head_ref, first_layer, i % tiles_per_seq)
    y_a = jnp.dot(ya_ref[...], wa_ref[...], preferred_element_type=F32)
    parts = [jnp.dot(pooled_ref[:, gi * POOL_GDIM:(gi + 1) * POOL_GDIM], wp_ref[gi],
                     preferred_element_type=F32) for gi in range(len(POOL_WINDOWS))]
    yb_in = (jnp.concatenate(parts, axis=1) * ps_ref[...]).astype(BF16)
    y_b = jnp.dot(yb_in, wb_ref[...], preferred_element_type=F32)
    m = (ga_ref[...].astype(F32) * y_a + gb_ref[...].astype(F32) * y_b).astype(BF16)
    h_new = h + jnp.dot(m, wo_ref[...], preferred_element_type=F32)
    row = _row_in_seq(i, tiles_per_seq, h_new.shape)
    out_ref[...] = jnp.where(row >= PAD, h_new, 0.0)


def _merge(ya, pooled, ga, gb, h, head, wa, wp, ps, wb, wo, tiles_per_seq, layer, seq):
    ntok = ya.shape[0]
    first_layer = layer == 0
    tok = pl.BlockSpec((TM, D_MODEL), lambda i: (i, 0))
    sq = _layer_spec((D_MODEL, D_MODEL), layer)
    return pl.pallas_call(
        functools.partial(_merge_kernel, tiles_per_seq, first_layer),
        out_shape=jax.ShapeDtypeStruct((ntok, D_MODEL), F32),
        grid=(ntok // TM,),
        in_specs=[tok, tok, tok, tok] + _resid_specs(first_layer, tiles_per_seq, seq) + [
            _const_spec((LEAD, D_MODEL)), sq,
            _layer_spec((len(POOL_WINDOWS), POOL_GDIM, POOL_GDIM), layer), _const_spec((1, D_MODEL)), sq, sq],
        out_specs=tok,
        compiler_params=pltpu.CompilerParams(dimension_semantics=("arbitrary",),
                                             vmem_limit_bytes=VMEM_LIMIT),
        name="merge",
    )(ya, pooled, ga, gb, h, h, h, head, wa, wp, ps, wb, wo)


def _ffn_kernel(tiles_per_seq, seq, last_layer, h_ref, halo_ref, g2_ref, wup_ref, cw_ref, cb_ref, wdn_ref, gf_ref,
                out_ref, slab_ref, *scratch):
    i = pl.program_id(0)
    tm = h_ref.shape[0]
    n_rows = tm + HALO
    n_vrow = n_rows // SUBLANES
    n_slab = D_MODEL // LANES
    n_chunks = D_FF // FF_CHUNK

    for c in range(n_slab):
        lanes = slice(c * LANES, (c + 1) * LANES)
        slab_ref[c, 0:tm, :] = h_ref[:, lanes]
        slab_ref[c, tm:n_rows, :] = halo_ref[:, lanes]
    h = jnp.concatenate(
        [jnp.concatenate([slab_ref[c, pl.ds(j, SUBLANES, stride=n_vrow), :] for c in range(n_slab)], axis=1)
         for j in range(n_vrow)], axis=0)
    hn = _rms(h, g2_ref[...]).astype(BF16)

    def pair(ref, j, rows=slice(None)):
        return jnp.concatenate([ref[rows, j * FF_CHUNK:(j + 1) * FF_CHUNK],
                                ref[rows, D_FF + j * FF_CHUNK:D_FF + (j + 1) * FF_CHUNK]], axis=1)

    def up_proj(j):
        return jnp.dot(hn, pair(wup_ref, j), preferred_element_type=F32)

    def earlier(x, k):
        cut = (n_vrow - k) * SUBLANES
        wrapped = [pltpu.roll(x[cut + r * SUBLANES:cut + (r + 1) * SUBLANES], 1, axis=0) for r in range(k)]
        return jnp.concatenate(wrapped + [x[:cut]], axis=0)

    def activation(up, j):
        c = pair(cb_ref, j) + pair(cw_ref, j, slice(CONV_W - 1, CONV_W)) * up
        for k in range(1, CONV_W):
            tap = CONV_W - 1 - k
            c = c + pair(cw_ref, j, slice(tap, tap + 1)) * earlier(up, k)
        a, bv = c[:, :FF_CHUNK], c[:, FF_CHUNK:]
        return (a * jax.nn.sigmoid(a) * bv).astype(BF16)

    def down_proj(acts, j_end):
        k1 = j_end * FF_CHUNK
        lhs = acts[0] if len(acts) == 1 else jnp.concatenate(acts, axis=1)
        return jnp.dot(lhs, wdn_ref[k1 - lhs.shape[1]:k1, :], preferred_element_type=F32)

    group_ends = []
    for size in FF_GROUPS:
        group_ends.append((group_ends[-1] if group_ends else 0) + size)
    assert group_ends[-1] == n_chunks

    acc = h
    up = up_proj(0)
    acts, pending = [], None
    for j in range(n_chunks):
        up_next = up_proj(j + 1) if j + 1 < n_chunks else None
        if pending is not None:
            acc = acc + down_proj(*pending)
            pending = None
        acts.append(activation(up, j))
        up = up_next
        if j + 1 in group_ends:
            pending, acts = (acts, j + 1), []
    acc = acc + down_proj(*pending)

    res = _rms(acc, gf_ref[...]) if last_layer else acc
    for j in range(n_vrow):
        for c in range(n_slab):
            slab_ref[c, pl.ds(j, SUBLANES, stride=n_vrow), :] = res[j * SUBLANES:(j + 1) * SUBLANES,
                                                                     c * LANES:(c + 1) * LANES]

    if not last_layer:
        row = _row_in_seq(i, tiles_per_seq, (tm, LANES))
        for c in range(n_slab):
            out_ref[:, c * LANES:(c + 1) * LANES] = jnp.where(row >= PAD, slab_ref[c, 0:tm, :], 0.0)
        return

    obuf_ref, sem_ref = scratch
    n_steps = pl.num_programs(0)

    def tile_copy(step, head_tile):
        slot, b, t = step % 2, step // tiles_per_seq, step % tiles_per_seq
        if head_tile:
            src = obuf_ref.at[slot, pl.ds(LEAD, tm - LEAD)]
            dst = out_ref.at[pl.ds(b * seq, tm - LEAD)]
        else:
            src = obuf_ref.at[slot]
            dst = out_ref.at[pl.ds(b * seq + t * tm - LEAD, tm)]
        return pltpu.make_async_copy(src, dst, sem_ref.at[slot])

    def for_tile(step, action):
        head_tile = step % tiles_per_seq == 0

        @pl.when(head_tile)
        def _():
            action(tile_copy(step, True))

        @pl.when(jnp.logical_not(head_tile))
        def _():
            action(tile_copy(step, False))

    @pl.when(i >= 2)
    def _():
        for_tile(i - 2, lambda cp: cp.wait())

    for c in range(n_slab):
        obuf_ref[i % 2, :, c * LANES:(c + 1) * LANES] = slab_ref[c, 0:tm, :]
    for_tile(i, lambda cp: cp.start())

    @pl.when(i == n_steps - 1)
    def _():
        @pl.when(i >= 1)
        def _():
            for_tile(i - 1, lambda cp: cp.wait())
        for_tile(i, lambda cp: cp.wait())


def _ffn(h, g2, wup, cw, cb, wdn, gf, tiles_per_seq, seq, layer, last_layer):
    ntok = h.shape[0]
    tok = pl.BlockSpec((TM, D_MODEL), lambda i: (i, 0))
    halo = pl.BlockSpec((HALO, D_MODEL), lambda i: (jnp.maximum(i * (TM // HALO) - 1, 0), 0))
    slab = pltpu.VMEM((D_MODEL // LANES, TM + HALO, LANES), F32)
    if last_layer:
        out_rows = ntok // (LEAD + seq) * seq
        out_spec = pl.BlockSpec(memory_space=pl.ANY)
        scratch = [slab, pltpu.VMEM((2, TM, D_MODEL), F32), pltpu.SemaphoreType.DMA((2,))]
    else:
        out_rows, out_spec, scratch = ntok, tok, [slab]
    return pl.pallas_call(
        functools.partial(_ffn_kernel, tiles_per_seq, seq, last_layer),
        out_shape=jax.ShapeDtypeStruct((out_rows, D_MODEL), F32),
        grid=(ntok // TM,),
        in_specs=[tok, halo, _const_spec((1, D_MODEL)), _layer_spec((D_MODEL, 2 * D_FF), layer),
                  _const_spec((CONV_W, 2 * D_FF)), _const_spec((1, 2 * D_FF)), _layer_spec((D_FF, D_MODEL), layer),
                  _const_spec((1, D_MODEL))],
        out_specs=out_spec,
        scratch_shapes=scratch,
        compiler_params=pltpu.CompilerParams(dimension_semantics=("arbitrary",),
                                             vmem_limit_bytes=VMEM_LIMIT),
        name="ffn",
    )(h, h, g2, wup, cw, cb, wdn, gf)


def kernel(x, meta_tokens, norm1_g, w_in, w_gk, b_gk, gla_norm_g, w_a, w_pool_grp, pool_scale,
           w_b, b_gates, w_o, norm2_g, w_up, conv_w, conv_b, w_down, final_norm_g):
    batch, seq, d = x.shape
    depth = w_in.shape[0]
    lp = LEAD + seq
    assert d == D_MODEL and lp % TM == 0 and TM % LEAD == 0 and LEAD % CHUNK == 0 and seq % LEAD == 0
    tiles_per_seq = lp // TM
    ntok = batch * lp

    head = jnp.concatenate([jnp.zeros((PAD, d), F32), meta_tokens.astype(F32)], axis=0)
    h = x.astype(F32).reshape(batch * seq, d)

    glr0 = 2 * GLA_DK + GLA_DV
    w_in16 = w_in.astype(BF16)
    w_r = jnp.concatenate(
        [w_in16[:, :, :glr0], w_in16[:, :, glr0 + GLA_RANK:], w_in16[:, :, glr0:glr0 + GLA_RANK],
         jnp.zeros((depth, d, LANES - GLA_RANK), BF16)], axis=2)
    wgk_pad = jnp.concatenate(
        [w_gk.astype(BF16), jnp.zeros((depth, LANES - GLA_RANK, GLA_DK), BF16)], axis=1)
    w_a16, w_b16, w_o16, w_pool16 = (w.astype(BF16) for w in (w_a, w_b, w_o, w_pool_grp))
    w_up16, w_down16 = w_up.astype(BF16), w_down.astype(BF16)

    for l in range(depth):
        q, k, v, g, sr, pooled, ga, gb = _inproj(
            h, head, norm1_g[l][None, :], w_r, wgk_pad, b_gk[l][None, :], b_gates[l][None, :],
            ntok, tiles_per_seq, l, seq)
        ya = _gla(q, k, v, g, sr, gla_norm_g[l][None, :], batch, tiles_per_seq)
        h = _merge(ya, pooled, ga, gb, h, head, w_a16, w_pool16, pool_scale[l][None, :], w_b16, w_o16,
                   tiles_per_seq, l, seq)
        h = _ffn(h, norm2_g[l][None, :], w_up16, conv_w[l], conv_b[l][None, :], w_down16,
                 final_norm_g[None, :], tiles_per_seq, seq, l, l == depth - 1)

    return h.reshape(batch, seq, d)
```

```python
import functools

import jax
import jax.numpy as jnp
from jax import lax
from jax.experimental import pallas as pl
from jax.experimental.pallas import tpu as pltpu

F32 = jnp.float32
BF16 = jnp.bfloat16

D_MODEL = 1024
N_META = 16
GLA_HEADS = 4
GLA_HK = 128
GLA_HV = 256
GLA_DK = GLA_HEADS * GLA_HK
GLA_DV = GLA_HEADS * GLA_HV
GLA_RANK = 16
GLA_TAU = 16.0
CHUNK = 64
POOL_WINDOWS = (2, 4, 8, 16)
POOL_GDIM = 256
D_FF = 2816
CONV_W = 3
EPS = 1e-6

PAD = 240
HALO = 16
TM = 768
LEAD = PAD + N_META
PARTS = TM // LEAD
INPROJ_COLS = 1024
FF_CHUNK = 256
FF_GROUPS = (4, 4, 2, 1)
LANES = 128
SUBLANES = 8
VMEM_LIMIT = 60 * 1024 * 1024

_Q0, _K0, _V0, _R0, _U0, _GA0, _GB0, _GLR0 = 0, 512, 1024, 2048, 3072, 4096, 5120, 6144
IN_COLS = _GLR0 + LANES


def _rms(x, g):
    ms = jnp.mean(x * x, axis=-1, keepdims=True)
    return x * lax.rsqrt(ms + EPS) * g


def _row_in_seq(tile_idx, tiles_per_seq, shape):
    base = (tile_idx % tiles_per_seq) * shape[0]
    return base + lax.broadcasted_iota(jnp.int32, shape, 0)


def _const_spec(shape):
    nd = len(shape)
    return pl.BlockSpec(shape, lambda *_: (0,) * nd, pipeline_mode=pl.Buffered(1))


def _layer_spec(shape, layer):
    nd = len(shape)
    return pl.BlockSpec((None,) + tuple(shape), lambda *_: (layer,) + (0,) * nd, pipeline_mode=pl.Buffered(1))


def _resid_specs(first_layer, tiles_per_seq, seq):
    def spec(p):
        if first_layer:
            def imap(i):
                b, t = i // tiles_per_seq, i % tiles_per_seq
                return (jnp.maximum(b * (seq // LEAD) + PARTS * t - 1 + p, 0), 0)
        else:
            imap = lambda i: (PARTS * i + p, 0)
        return pl.BlockSpec((LEAD, D_MODEL), imap)
    return [spec(p) for p in range(PARTS)]


def _load_resid(part_refs, head_ref, first_layer, tile_in_seq):
    parts = [r[...] for r in part_refs]
    if first_layer:
        parts[0] = jnp.where(tile_in_seq == 0, head_ref[...], parts[0])
    return jnp.concatenate(parts, axis=0)


def _inproj_kernel(tiles_per_seq, first_layer, h0_ref, h1_ref, h2_ref, head_ref, g1_ref, w_ref, wgk_ref, bgk_ref,
                   bg_ref, q_ref, k_ref, v_ref, g_ref, sr_ref, pooled_ref, ga_ref, gb_ref, carry_ref):
    i = pl.program_id(0)
    tm = TM

    @pl.when(i == 0)
    def _():
        carry_ref[...] = jnp.zeros_like(carry_ref)

    h = _load_resid((h0_ref, h1_ref, h2_ref), head_ref, first_layer, i % tiles_per_seq)
    hn = _rms(h, g1_ref[...]).astype(BF16)

    half = INPROJ_COLS
    head_pos = (i % tiles_per_seq) * tm + lax.broadcasted_iota(jnp.int32, (HALO, POOL_GDIM), 0)
    stash = {}

    def proj(c0, c1):
        return lambda: jnp.dot(hn, w_ref[:, c0:c1], preferred_element_type=F32)

    def pool_epilogue(first_group):
        def fn(u):
            c0 = first_group * POOL_GDIM
            ext = jnp.concatenate([carry_ref[:, c0:c0 + half], u], axis=0)
            carry_ref[:, c0:c0 + half] = u[tm - HALO:, :]
            for gl in range(half // POOL_GDIM):
                w = POOL_WINDOWS[first_group + gl]
                loc = slice(gl * POOL_GDIM, (gl + 1) * POOL_GDIM)
                s = ext[:, loc]
                sh = 1
                while sh < w:
                    s = s + pltpu.roll(s, sh, axis=0)
                    sh *= 2
                out_cols = slice(c0 + gl * POOL_GDIM, c0 + (gl + 1) * POOL_GDIM)
                mean = s[HALO:, :] * (1.0 / w)
                pooled_ref[:, out_cols] = (mean - u[:, loc]).astype(BF16)
                cnt = jnp.clip(head_pos + 1, 1, w).astype(F32)
                pooled_ref[PAD:PAD + HALO, out_cols] = (
                    s[HALO + PAD:2 * HALO + PAD, :] / cnt - u[PAD:PAD + HALO, loc]).astype(BF16)
        return fn

    def glr_epilogue(x):
        stash["glr"] = x.astype(BF16)

    def forget_gate_epilogue(z):
        z = z + bgk_ref[...]
        g_ref[...] = (jnp.minimum(z, 0.0) - jnp.log(1.0 + jnp.exp(-jnp.abs(z)))) * (1.0 / GLA_TAU)

    def gate_epilogue(out_ref, bias0, c0):
        def fn(x):
            out_ref[:, c0:c0 + half] = jax.nn.sigmoid(x + bg_ref[:, bias0 + c0:bias0 + c0 + half]).astype(BF16)
        return fn

    def silu_epilogue(c0):
        def fn(x):
            sr_ref[:, c0:c0 + half] = (x * jax.nn.sigmoid(x)).astype(BF16)
        return fn

    def cast_epilogue(out_ref, c0):
        def fn(x):
            out_ref[:, c0:c0 + half] = x.astype(BF16)
        return fn

    assert half == D_MODEL == GLA_DV == 2 * GLA_DK

    def qk_epilogue(x):
        q_ref[...] = (x[:, :GLA_DK] * GLA_HK ** -0.5).astype(BF16)
        k_ref[...] = x[:, GLA_DK:].astype(BF16)

    stages = [
        (proj(_GLR0, IN_COLS), glr_epilogue),
        (proj(_U0, _GA0), pool_epilogue(0)),
        (proj(_GA0, _GB0), gate_epilogue(ga_ref, 0, 0)),
        (lambda: jnp.dot(stash["glr"], wgk_ref[...], preferred_element_type=F32), forget_gate_epilogue),
        (proj(_GB0, _GLR0), gate_epilogue(gb_ref, D_MODEL, 0)),
        (proj(_R0, _U0), silu_epilogue(0)),
        (proj(_V0, _R0), cast_epilogue(v_ref, 0)),
        (proj(_Q0, _V0), qk_epilogue),
    ]

    res = stages[0][0]()
    for idx, (_, epilogue) in enumerate(stages):
        nxt = stages[idx + 1][0]() if idx + 1 < len(stages) else None
        epilogue(res)
        res = nxt


def _inproj(h, head, g1, w_r, wgk_pad, bgk, bgates, ntok, tiles_per_seq, layer, seq):
    first_layer = layer == 0
    tok = lambda n: pl.BlockSpec((TM, n), lambda i: (i, 0))
    out_shape = (
        jax.ShapeDtypeStruct((ntok, GLA_DK), BF16),
        jax.ShapeDtypeStruct((ntok, GLA_DK), BF16),
        jax.ShapeDtypeStruct((ntok, GLA_DV), BF16),
        jax.ShapeDtypeStruct((ntok, GLA_DK), F32),
        jax.ShapeDtypeStruct((ntok, GLA_DV), BF16),
        jax.ShapeDtypeStruct((ntok, D_MODEL), BF16),
        jax.ShapeDtypeStruct((ntok, D_MODEL), BF16),
        jax.ShapeDtypeStruct((ntok, D_MODEL), BF16),
    )
    return pl.pallas_call(
        functools.partial(_inproj_kernel, tiles_per_seq, first_layer),
        out_shape=out_shape,
        grid=(ntok // TM,),
        in_specs=_resid_specs(first_layer, tiles_per_seq, seq) + [
            _const_spec((LEAD, D_MODEL)), _const_spec((1, D_MODEL)), _layer_spec((D_MODEL, IN_COLS), layer),
            _layer_spec((LANES, GLA_DK), layer), _const_spec((1, GLA_DK)), _const_spec((1, 2 * D_MODEL))],
        out_specs=(tok(GLA_DK), tok(GLA_DK), tok(GLA_DV), tok(GLA_DK), tok(GLA_DV),
                   tok(D_MODEL), tok(D_MODEL), tok(D_MODEL)),
        scratch_shapes=[pltpu.VMEM((HALO, D_MODEL), F32)],
        compiler_params=pltpu.CompilerParams(dimension_semantics=("arbitrary",),
                                             vmem_limit_bytes=VMEM_LIMIT),
        name="inproj",
    )(h, h, h, head, g1, w_r, wgk_pad, bgk, bgates)


def _gla_kernel(q_ref, k_ref, v_ref, g_ref, sr_ref, gn_ref, y_ref, state_ref, o_ref):
    t = pl.program_id(1)
    tg = q_ref.shape[0]

    @pl.when(t == 0)
    def _():
        state_ref[...] = jnp.zeros_like(state_ref)

    rr = lax.broadcasted_iota(jnp.int32, (CHUNK, CHUNK), 0)
    cc = lax.broadcasted_iota(jnp.int32, (CHUNK, CHUNK), 1)
    causal = rr >= cc
    tril = causal.astype(BF16)

    n_chunks = tg // CHUNK
    units = [(c, h) for c in range(n_chunks) for h in range(GLA_HEADS)]
    rows = lambda c: slice(c * CHUNK, (c + 1) * CHUNK)
    kcols = lambda h: slice(h * GLA_HK, (h + 1) * GLA_HK)
    vcols = lambda h: slice(h * GLA_HV, (h + 1) * GLA_HV)

    g = g_ref[...]
    g_hi = g.astype(BF16)
    g_lo = (g - g_hi.astype(F32)).astype(BF16)
    b_chunks = [jnp.dot(tril, g_hi[rows(c)], preferred_element_type=F32)
                + jnp.dot(tril, g_lo[rows(c)], preferred_element_type=F32) for c in range(n_chunks)]
    b = jnp.concatenate(b_chunks, axis=0)
    b_last = [bc[CHUNK - 1:CHUNK, :] for bc in b_chunks]
    b_end = jnp.concatenate([jnp.broadcast_to(bl, (CHUNK, GLA_DK)) for bl in b_last], axis=0)
    q = q_ref[...].astype(F32)
    k = k_ref[...].astype(F32)
    q_dec = (q * jnp.exp(b)).astype(BF16)
    k_inv = (k * jnp.exp(-b)).astype(BF16)
    k_end = (k * jnp.exp(b_end - b)).astype(BF16)
    dec = [jnp.exp(bl) for bl in b_last]

    att = {}
    for c, h in units:
        a = lax.dot_general(q_dec[rows(c), kcols(h)], k_inv[rows(c), kcols(h)], (((1,), (1,)), ((), ())),
                            preferred_element_type=F32)
        att[c, h] = jnp.where(causal, a, 0.0).astype(BF16)
    for c, h in units:
        o_ref[rows(c), vcols(h)] = jnp.dot(att[c, h], v_ref[rows(c), vcols(h)], preferred_element_type=F32)
    d_state = {}
    for c, h in units:
        d_state[c, h] = lax.dot_general(k_end[rows(c), kcols(h)], v_ref[rows(c), vcols(h)],
                                        (((0,), (0,)), ((), ())), preferred_element_type=F32)

    state = [state_ref[h] for h in range(GLA_HEADS)]
    for c, h in units:
        o_ref[rows(c), vcols(h)] += jnp.dot(q_dec[rows(c), kcols(h)], state[h].astype(BF16),
                                            preferred_element_type=F32)
        dec_col = jnp.broadcast_to(dec[c][:, kcols(h)], (GLA_HK, GLA_HK)).T
        state[h] = jnp.concatenate([dec_col, dec_col], axis=1) * state[h] + d_state[c, h]
    for h in range(GLA_HEADS):
        state_ref[h] = state[h]

    for h in range(GLA_HEADS):
        vs = slice(h * GLA_HV, (h + 1) * GLA_HV)
        y_ref[:, vs] = (_rms(o_ref[:, vs], gn_ref[...]) * sr_ref[:, vs].astype(F32)).astype(BF16)


def _gla(q, k, v, g, sr, gn, batch, tiles_per_seq):
    ntok = q.shape[0]
    tok = lambda n: pl.BlockSpec((TM, n), lambda b, t: (b * tiles_per_seq + t, 0))
    return pl.pallas_call(
        _gla_kernel,
        out_shape=jax.ShapeDtypeStruct((ntok, GLA_DV), BF16),
        grid=(batch, tiles_per_seq),
        in_specs=[tok(GLA_DK), tok(GLA_DK), tok(GLA_DV), tok(GLA_DK), tok(GLA_DV), _const_spec((1, GLA_HV))],
        out_specs=tok(GLA_DV),
        scratch_shapes=[pltpu.VMEM((GLA_HEADS, GLA_HK, GLA_HV), F32), pltpu.VMEM((TM, GLA_DV), F32)],
        compiler_params=pltpu.CompilerParams(dimension_semantics=("arbitrary", "arbitrary"),
                                             vmem_limit_bytes=VMEM_LIMIT),
        name="gla",
    )(q, k, v, g, sr, gn)


def _merge_kernel(tiles_per_seq, first_layer, ya_ref, pooled_ref, ga_ref, gb_ref, h0_ref, h1_ref, h2_ref, head_ref,
                  wa_ref, wp_ref, ps_ref, wb_ref, wo_ref, out_ref):
    i = pl.program_id(0)
    h = _load_resid((h0_ref, h1_ref, h2_ref), head_ref, first_layer, i % tiles_per_seq)
    y_a = jnp.dot(ya_ref[...], wa_ref[...], preferred_element_type=F32)
    parts = [jnp.dot(pooled_ref[:, gi * POOL_GDIM:(gi + 1) * POOL_GDIM], wp_ref[gi],
                     preferred_element_type=F32) for gi in range(len(POOL_WINDOWS))]
    yb_in = (jnp.concatenate(parts, axis=1) * ps_ref[...]).astype(BF16)
    y_b = jnp.dot(yb_in, wb_ref[...], preferred_element_type=F32)
    m = (ga_ref[...].astype(F32) * y_a + gb_ref[...].astype(F32) * y_b).astype(BF16)
    h_new = h + jnp.dot(m, wo_ref[...], preferred_element_type=F32)
    row = _row_in_seq(i, tiles_per_seq, h_new.shape)
    out_ref[...] = jnp.where(row >= PAD, h_new, 0.0)


def _merge(ya, pooled, ga, gb, h, head, wa, wp, ps, wb, wo, tiles_per_seq, layer, seq):
    ntok = ya.shape[0]
    first_layer = layer == 0
    tok = pl.BlockSpec((TM, D_MODEL), lambda i: (i, 0))
    sq = _layer_spec((D_MODEL, D_MODEL), layer)
    return pl.pallas_call(
        functools.partial(_merge_kernel, tiles_per_seq, first_layer),
        out_shape=jax.ShapeDtypeStruct((ntok, D_MODEL), F32),
        grid=(ntok // TM,),
        in_specs=[tok, tok, tok, tok] + _resid_specs(first_layer, tiles_per_seq, seq) + [
            _const_spec((LEAD, D_MODEL)), sq,
            _layer_spec((len(POOL_WINDOWS), POOL_GDIM, POOL_GDIM), layer), _const_spec((1, D_MODEL)), sq, sq],
        out_specs=tok,
        compiler_params=pltpu.CompilerParams(dimension_semantics=("arbitrary",),
                                             vmem_limit_bytes=VMEM_LIMIT),
        name="merge",
    )(ya, pooled, ga, gb, h, h, h, head, wa, wp, ps, wb, wo)


def _ffn_kernel(tiles_per_seq, seq, last_layer, h_ref, halo_ref, g2_ref, wup_ref, cw_ref, cb_ref, wdn_ref, gf_ref,
                out_ref, slab_ref, *scratch):
    i = pl.program_id(0)
    tm = h_ref.shape[0]
    n_rows = tm + HALO
    n_vrow = n_rows // SUBLANES
    n_slab = D_MODEL // LANES
    n_chunks = D_FF // FF_CHUNK

    for c in range(n_slab):
        lanes = slice(c * LANES, (c + 1) * LANES)
        slab_ref[c, 0:tm, :] = h_ref[:, lanes]
        slab_ref[c, tm:n_rows, :] = halo_ref[:, lanes]
    h = jnp.concatenate(
        [jnp.concatenate([slab_ref[c, pl.ds(j, SUBLANES, stride=n_vrow), :] for c in range(n_slab)], axis=1)
         for j in range(n_vrow)], axis=0)
    hn = _rms(h, g2_ref[...]).astype(BF16)

    def pair(ref, j, rows=slice(None)):
        return jnp.concatenate([ref[rows, j * FF_CHUNK:(j + 1) * FF_CHUNK],
                                ref[rows, D_FF + j * FF_CHUNK:D_FF + (j + 1) * FF_CHUNK]], axis=1)

    def up_proj(j):
        return jnp.dot(hn, pair(wup_ref, j), preferred_element_type=F32)

    def earlier(x, k):
        cut = (n_vrow - k) * SUBLANES
        wrapped = [pltpu.roll(x[cut + r * SUBLANES:cut + (r + 1) * SUBLANES], 1, axis=0) for r in range(k)]
        return jnp.concatenate(wrapped + [x[:cut]], axis=0)

    def activation(up, j):
        c = pair(cb_ref, j) + pair(cw_ref, j, slice(CONV_W - 1, CONV_W)) * up
        for k in range(1, CONV_W):
            tap = CONV_W - 1 - k
            c = c + pair(cw_ref, j, slice(tap, tap + 1)) * earlier(up, k)
        a, bv = c[:, :FF_CHUNK], c[:, FF_CHUNK:]
        return (a * jax.nn.sigmoid(a) * bv).astype(BF16)

    def down_proj(acts, j_end):
        k1 = j_end * FF_CHUNK
        lhs = acts[0] if len(acts) == 1 else jnp.concatenate(acts, axis=1)
        return jnp.dot(lhs, wdn_ref[k1 - lhs.shape[1]:k1, :], preferred_element_type=F32)

    group_ends = []
    for size in FF_GROUPS:
        group_ends.append((group_ends[-1] if group_ends else 0) + size)
    assert group_ends[-1] == n_chunks

    acc = h
    up = up_proj(0)
    acts, pending = [], None
    for j in range(n_chunks):
        up_next = up_proj(j + 1) if j + 1 < n_chunks else None
        if pending is not None:
            acc = acc + down_proj(*pending)
            pending = None
        acts.append(activation(up, j))
        up = up_next
        if j + 1 in group_ends:
            pending, acts = (acts, j + 1), []
    acc = acc + down_proj(*pending)

    res = _rms(acc, gf_ref[...]) if last_layer else acc
    for j in range(n_vrow):
        for c in range(n_slab):
            slab_ref[c, pl.ds(j, SUBLANES, stride=n_vrow), :] = res[j * SUBLANES:(j + 1) * SUBLANES,
                                                                     c * LANES:(c + 1) * LANES]

    if not last_layer:
        row = _row_in_seq(i, tiles_per_seq, (tm, LANES))
        for c in range(n_slab):
            out_ref[:, c * LANES:(c + 1) * LANES] = jnp.where(row >= PAD, slab_ref[c, 0:tm, :], 0.0)
        return

    obuf_ref, sem_ref = scratch
    n_steps = pl.num_programs(0)

    def tile_copy(step, head_tile):
        slot, b, t = step % 2, step // tiles_per_seq, step % tiles_per_seq
        if head_tile:
            src = obuf_ref.at[slot, pl.ds(LEAD, tm - LEAD)]
            dst = out_ref.at[pl.ds(b * seq, tm - LEAD)]
        else:
            src = obuf_ref.at[slot]
            dst = out_ref.at[pl.ds(b * seq + t * tm - LEAD, tm)]
        return pltpu.make_async_copy(src, dst, sem_ref.at[slot])

    def for_tile(step, action):
        head_tile = step % tiles_per_seq == 0

        @pl.when(head_tile)
        def _():
            action(tile_copy(step, True))

        @pl.when(jnp.logical_not(head_tile))
        def _():
            action(tile_copy(step, False))

    @pl.when(i >= 2)
    def _():
        for_tile(i - 2, lambda cp: cp.wait())

    for c in range(n_slab):
        obuf_ref[i % 2, :, c * LANES:(c + 1) * LANES] = slab_ref[c, 0:tm, :]
    for_tile(i, lambda cp: cp.start())

    @pl.when(i == n_steps - 1)
    def _():
        @pl.when(i >= 1)
        def _():
            for_tile(i - 1, lambda cp: cp.wait())
        for_tile(i, lambda cp: cp.wait())


def _ffn(h, g2, wup, cw, cb, wdn, gf, tiles_per_seq, seq, layer, last_layer):
    ntok = h.shape[0]
    tok = pl.BlockSpec((TM, D_MODEL), lambda i: (i, 0))
    halo = pl.BlockSpec((HALO, D_MODEL), lambda i: (jnp.maximum(i * (TM // HALO) - 1, 0), 0))
    slab = pltpu.VMEM((D_MODEL // LANES, TM + HALO, LANES), F32)
    if last_layer:
        out_rows = ntok // (LEAD + seq) * seq
        out_spec = pl.BlockSpec(memory_space=pl.ANY)
        scratch = [slab, pltpu.VMEM((2, TM, D_MODEL), F32), pltpu.SemaphoreType.DMA((2,))]
    else:
        out_rows, out_spec, scratch = ntok, tok, [slab]
    return pl.pallas_call(
        functools.partial(_ffn_kernel, tiles_per_seq, seq, last_layer),
        out_shape=jax.ShapeDtypeStruct((out_rows, D_MODEL), F32),
        grid=(ntok // TM,),
        in_specs=[tok, halo, _const_spec((1, D_MODEL)), _layer_spec((D_MODEL, 2 * D_FF), layer),
                  _const_spec((CONV_W, 2 * D_FF)), _const_spec((1, 2 * D_FF)), _layer_spec((D_FF, D_MODEL), layer),
                  _const_spec((1, D_MODEL))],
        out_specs=out_spec,
        scratch_shapes=scratch,
        compiler_params=pltpu.CompilerParams(dimension_semantics=("arbitrary",),
                                             vmem_limit_bytes=VMEM_LIMIT),
        name="ffn",
    )(h, h, g2, wup, cw, cb, wdn, gf)


def _regroup_kernel(w_ref, out_ref):
    x = w_ref[...]
    glr0 = _R0
    out_ref[:, :glr0] = x[:, :glr0].astype(BF16)
    out_ref[:, glr0:_GLR0] = x[:, glr0 + GLA_RANK:].astype(BF16)
    tail = jnp.concatenate([x[:, glr0:glr0 + GLA_RANK], jnp.zeros((x.shape[0], LANES - GLA_RANK), x.dtype)], axis=1)
    out_ref[:, _GLR0:] = tail.astype(BF16)


def _regroup_inproj(w_in):
    depth, d, cols = w_in.shape
    rows = 256
    return pl.pallas_call(
        _regroup_kernel,
        out_shape=jax.ShapeDtypeStruct((depth, d, IN_COLS), BF16),
        grid=(depth, d // rows),
        in_specs=[pl.BlockSpec((None, rows, cols), lambda l, i: (l, i, 0))],
        out_specs=pl.BlockSpec((None, rows, IN_COLS), lambda l, i: (l, i, 0)),
        compiler_params=pltpu.CompilerParams(dimension_semantics=("arbitrary", "arbitrary")),
        name="regroup_inproj",
    )(w_in)


def kernel(x, meta_tokens, norm1_g, w_in, w_gk, b_gk, gla_norm_g, w_a, w_pool_grp, pool_scale,
           w_b, b_gates, w_o, norm2_g, w_up, conv_w, conv_b, w_down, final_norm_g):
    batch, seq, d = x.shape
    depth = w_in.shape[0]
    lp = LEAD + seq
    assert d == D_MODEL and lp % TM == 0 and TM % LEAD == 0 and LEAD % CHUNK == 0 and seq % LEAD == 0
    tiles_per_seq = lp // TM
    ntok = batch * lp

    head = jnp.concatenate([jnp.zeros((PAD, d), F32), meta_tokens.astype(F32)], axis=0)
    h = x.astype(F32).reshape(batch * seq, d)

    assert w_in.shape[2] == _GLR0 + GLA_RANK and _R0 == 2 * GLA_DK + GLA_DV
    w_r = _regroup_inproj(w_in)
    wgk_pad = jnp.concatenate(
        [w_gk.astype(BF16), jnp.zeros((depth, LANES - GLA_RANK, GLA_DK), BF16)], axis=1)
    w_a16, w_b16, w_o16, w_pool16 = (w.astype(BF16) for w in (w_a, w_b, w_o, w_pool_grp))
    w_up16, w_down16 = w_up.astype(BF16), w_down.astype(BF16)

    for l in range(depth):
        q, k, v, g, sr, pooled, ga, gb = _inproj(
            h, head, norm1_g[l][None, :], w_r, wgk_pad, b_gk[l][None, :], b_gates[l][None, :],
            ntok, tiles_per_seq, l, seq)
        ya = _gla(q, k, v, g, sr, gla_norm_g[l][None, :], batch, tiles_per_seq)
        h = _merge(ya, pooled, ga, gb, h, head, w_a16, w_pool16, pool_scale[l][None, :], w_b16, w_o16,
                   tiles_per_seq, l, seq)
        h = _ffn(h, norm2_g[l][None, :], w_up16, conv_w[l], conv_b[l][None, :], w_down16,
                 final_norm_g[None, :], tiles_per_seq, seq, l, l == depth - 1)

    return h.reshape(batch, seq, d)
```

```python
import functools

import jax
import jax.numpy as jnp
from jax import lax
from jax.experimental import pallas as pl
from jax.experimental.pallas import tpu as pltpu

F32 = jnp.float32
BF16 = jnp.bfloat16

D_MODEL = 1024
N_META = 16
GLA_HEADS = 4
GLA_HK = 128
GLA_HV = 256
GLA_DK = GLA_HEADS * GLA_HK
GLA_DV = GLA_HEADS * GLA_HV
GLA_RANK = 16
GLA_TAU = 16.0
CHUNK = 64
POOL_WINDOWS = (2, 4, 8, 16)
POOL_GDIM = 256
D_FF = 2816
CONV_W = 3
EPS = 1e-6

PAD = 240
HALO = 16
TM = 768
LEAD = PAD + N_META
PARTS = TM // LEAD
INPROJ_COLS = 1024
FF_CHUNK = 256
FF_GROUPS = (4, 4, 2, 1)
LANES = 128
SUBLANES = 8
VMEM_LIMIT = 60 * 1024 * 1024

_Q0, _K0, _V0, _R0, _U0, _GA0, _GB0, _GLR0 = 0, 512, 1024, 2048, 3072, 4096, 5120, 6144
IN_COLS = _GLR0 + LANES


def _rms(x, g):
    ms = jnp.mean(x * x, axis=-1, keepdims=True)
    return x * lax.rsqrt(ms + EPS) * g


def _sigmoid(x):
    return 0.5 * jnp.tanh(0.5 * x) + 0.5


def _row_in_seq(tile_idx, tiles_per_seq, shape):
    base = (tile_idx % tiles_per_seq) * shape[0]
    return base + lax.broadcasted_iota(jnp.int32, shape, 0)


def _const_spec(shape):
    nd = len(shape)
    return pl.BlockSpec(shape, lambda *_: (0,) * nd, pipeline_mode=pl.Buffered(1))


def _layer_spec(shape, layer):
    nd = len(shape)
    return pl.BlockSpec((None,) + tuple(shape), lambda *_: (layer,) + (0,) * nd, pipeline_mode=pl.Buffered(1))


def _resid_specs(first_layer, tiles_per_seq, seq):
    def spec(p):
        if first_layer:
            def imap(i):
                b, t = i // tiles_per_seq, i % tiles_per_seq
                return (jnp.maximum(b * (seq // LEAD) + PARTS * t - 1 + p, 0), 0)
        else:
            imap = lambda i: (PARTS * i + p, 0)
        return pl.BlockSpec((LEAD, D_MODEL), imap)
    return [spec(p) for p in range(PARTS)]


def _load_resid(part_refs, head_ref, first_layer, tile_in_seq):
    parts = [r[...] for r in part_refs]
    if first_layer:
        parts[0] = jnp.where(tile_in_seq == 0, head_ref[...], parts[0])
    return jnp.concatenate(parts, axis=0)


def _inproj_kernel(tiles_per_seq, first_layer, h0_ref, h1_ref, h2_ref, head_ref, g1_ref, w_ref, wgk_ref, bgk_ref,
                   bg_ref, q_ref, k_ref, v_ref, g_ref, sr_ref, pooled_ref, ga_ref, gb_ref, carry_ref):
    i = pl.program_id(0)
    tm = TM

    @pl.when(i == 0)
    def _():
        carry_ref[...] = jnp.zeros_like(carry_ref)

    h = _load_resid((h0_ref, h1_ref, h2_ref), head_ref, first_layer, i % tiles_per_seq)
    hn = _rms(h, g1_ref[...]).astype(BF16)

    half = INPROJ_COLS
    head_pos = (i % tiles_per_seq) * tm + lax.broadcasted_iota(jnp.int32, (HALO, POOL_GDIM), 0)
    stash = {}

    def proj(c0, c1):
        return lambda: jnp.dot(hn, w_ref[:, c0:c1], preferred_element_type=F32)

    def pool_epilogue(first_group):
        def fn(u):
            c0 = first_group * POOL_GDIM
            ext = jnp.concatenate([carry_ref[:, c0:c0 + half], u], axis=0)
            carry_ref[:, c0:c0 + half] = u[tm - HALO:, :]
            for gl in range(half // POOL_GDIM):
                w = POOL_WINDOWS[first_group + gl]
                loc = slice(gl * POOL_GDIM, (gl + 1) * POOL_GDIM)
                s = ext[:, loc]
                sh = 1
                while sh < w:
                    s = s + pltpu.roll(s, sh, axis=0)
                    sh *= 2
                out_cols = slice(c0 + gl * POOL_GDIM, c0 + (gl + 1) * POOL_GDIM)
                mean = s[HALO:, :] * (1.0 / w)
                pooled_ref[:, out_cols] = (mean - u[:, loc]).astype(BF16)
                cnt = jnp.clip(head_pos + 1, 1, w).astype(F32)
                pooled_ref[PAD:PAD + HALO, out_cols] = (
                    s[HALO + PAD:2 * HALO + PAD, :] / cnt - u[PAD:PAD + HALO, loc]).astype(BF16)
        return fn

    def glr_epilogue(x):
        stash["glr"] = x.astype(BF16)

    def forget_gate_epilogue(z):
        z = z + bgk_ref[...]
        g_ref[...] = (jnp.minimum(z, 0.0) - jnp.log(1.0 + jnp.exp(-jnp.abs(z)))) * (1.0 / GLA_TAU)

    def gate_epilogue(out_ref, bias0, c0):
        def fn(x):
            out_ref[:, c0:c0 + half] = _sigmoid(x + bg_ref[:, bias0 + c0:bias0 + c0 + half]).astype(BF16)
        return fn

    def silu_epilogue(c0):
        def fn(x):
            sr_ref[:, c0:c0 + half] = (x * _sigmoid(x)).astype(BF16)
        return fn

    def cast_epilogue(out_ref, c0):
        def fn(x):
            out_ref[:, c0:c0 + half] = x.astype(BF16)
        return fn

    assert half == D_MODEL == GLA_DV == 2 * GLA_DK

    def qk_epilogue(x):
        q_ref[...] = (x[:, :GLA_DK] * GLA_HK ** -0.5).astype(BF16)
        k_ref[...] = x[:, GLA_DK:].astype(BF16)

    stages = [
        (proj(_GLR0, IN_COLS), glr_epilogue),
        (proj(_U0, _GA0), pool_epilogue(0)),
        (proj(_GA0, _GB0), gate_epilogue(ga_ref, 0, 0)),
        (lambda: jnp.dot(stash["glr"], wgk_ref[...], preferred_element_type=F32), forget_gate_epilogue),
        (proj(_GB0, _GLR0), gate_epilogue(gb_ref, D_MODEL, 0)),
        (proj(_R0, _U0), silu_epilogue(0)),
        (proj(_V0, _R0), cast_epilogue(v_ref, 0)),
        (proj(_Q0, _V0), qk_epilogue),
    ]

    res = stages[0][0]()
    for idx, (_, epilogue) in enumerate(stages):
        nxt = stages[idx + 1][0]() if idx + 1 < len(stages) else None
        epilogue(res)
        res = nxt


def _inproj(h, head, g1, w_r, wgk_pad, bgk, bgates, ntok, tiles_per_seq, layer, seq):
    first_layer = layer == 0
    tok = lambda n: pl.BlockSpec((TM, n), lambda i: (i, 0))
    out_shape = (
        jax.ShapeDtypeStruct((ntok, GLA_DK), BF16),
        jax.ShapeDtypeStruct((ntok, GLA_DK), BF16),
        jax.ShapeDtypeStruct((ntok, GLA_DV), BF16),
        jax.ShapeDtypeStruct((ntok, GLA_DK), F32),
        jax.ShapeDtypeStruct((ntok, GLA_DV), BF16),
        jax.ShapeDtypeStruct((ntok, D_MODEL), BF16),
        jax.ShapeDtypeStruct((ntok, D_MODEL), BF16),
        jax.ShapeDtypeStruct((ntok, D_MODEL), BF16),
    )
    return pl.pallas_call(
        functools.partial(_inproj_kernel, tiles_per_seq, first_layer),
        out_shape=out_shape,
        grid=(ntok // TM,),
        in_specs=_resid_specs(first_layer, tiles_per_seq, seq) + [
            _const_spec((LEAD, D_MODEL)), _const_spec((1, D_MODEL)), _layer_spec((D_MODEL, IN_COLS), layer),
            _layer_spec((LANES, GLA_DK), layer), _const_spec((1, GLA_DK)), _const_spec((1, 2 * D_MODEL))],
        out_specs=(tok(GLA_DK), tok(GLA_DK), tok(GLA_DV), tok(GLA_DK), tok(GLA_DV),
                   tok(D_MODEL), tok(D_MODEL), tok(D_MODEL)),
        scratch_shapes=[pltpu.VMEM((HALO, D_MODEL), F32)],
        compiler_params=pltpu.CompilerParams(dimension_semantics=("arbitrary",),
                                             vmem_limit_bytes=VMEM_LIMIT),
        name="inproj",
    )(h, h, h, head, g1, w_r, wgk_pad, bgk, bgates)


def _gla_kernel(q_ref, k_ref, v_ref, g_ref, sr_ref, gn_ref, y_ref, state_ref, o_ref):
    t = pl.program_id(1)
    tg = q_ref.shape[0]

    @pl.when(t == 0)
    def _():
        state_ref[...] = jnp.zeros_like(state_ref)

    rr = lax.broadcasted_iota(jnp.int32, (CHUNK, CHUNK), 0)
    cc = lax.broadcasted_iota(jnp.int32, (CHUNK, CHUNK), 1)
    causal = rr >= cc
    tril = causal.astype(BF16)

    n_chunks = tg // CHUNK
    rows = lambda c: slice(c * CHUNK, (c + 1) * CHUNK)
    kcols = lambda h: slice(h * GLA_HK, (h + 1) * GLA_HK)
    vcols = lambda h: slice(h * GLA_HV, (h + 1) * GLA_HV)

    g = g_ref[...]
    g_hi = g.astype(BF16)
    g_lo = (g - g_hi.astype(F32)).astype(BF16)
    b_chunks = [jnp.dot(tril, g_hi[rows(c)], preferred_element_type=F32)
                + jnp.dot(tril, g_lo[rows(c)], preferred_element_type=F32) for c in range(n_chunks)]

    q_dec, k_end, dec, att, d_state = {}, {}, {}, {}, {}
    state = [state_ref[h] for h in range(GLA_HEADS)]

    def score_stage(c):
        b = b_chunks[c]
        dec[c] = jnp.exp(b[CHUNK - 1:CHUNK, :])
        grow = jnp.exp(b)
        q_dec[c] = (q_ref[rows(c), :].astype(F32) * grow).astype(BF16)
        k_inv = k_ref[rows(c), :].astype(F32) * (1.0 / grow)
        k_end[c] = (k_inv * dec[c]).astype(BF16)
        k_inv = k_inv.astype(BF16)
        for h in range(GLA_HEADS):
            att[c, h] = lax.dot_general(q_dec[c][:, kcols(h)], k_inv[:, kcols(h)], (((1,), (1,)), ((), ())),
                                        preferred_element_type=F32)

    def increment_stage(c):
        for h in range(GLA_HEADS):
            att[c, h] = jnp.where(causal, att[c, h], 0.0).astype(BF16)
            d_state[c, h] = lax.dot_general(k_end[c][:, kcols(h)], v_ref[rows(c), vcols(h)],
                                            (((0,), (0,)), ((), ())), preferred_element_type=F32)

    def output_stage(c):
        for h in range(GLA_HEADS):
            o_ref[rows(c), vcols(h)] = (
                jnp.dot(att.pop((c, h)), v_ref[rows(c), vcols(h)], preferred_element_type=F32)
                + jnp.dot(q_dec[c][:, kcols(h)], state[h].astype(BF16), preferred_element_type=F32))
            dec_col = jnp.broadcast_to(dec[c][:, kcols(h)], (GLA_HK, GLA_HK)).T
            state[h] = jnp.concatenate([dec_col, dec_col], axis=1) * state[h] + d_state.pop((c, h))

    for step in range(n_chunks + 2):
        if step < n_chunks:
            score_stage(step)
        if 0 <= step - 1 < n_chunks:
            increment_stage(step - 1)
        if 0 <= step - 2 < n_chunks:
            output_stage(step - 2)
    for h in range(GLA_HEADS):
        state_ref[h] = state[h]

    for h in range(GLA_HEADS):
        vs = slice(h * GLA_HV, (h + 1) * GLA_HV)
        y_ref[:, vs] = (_rms(o_ref[:, vs], gn_ref[...]) * sr_ref[:, vs].astype(F32)).astype(BF16)


def _gla(q, k, v, g, sr, gn, batch, tiles_per_seq):
    ntok = q.shape[0]
    tok = lambda n: pl.BlockSpec((TM, n), lambda b, t: (b * tiles_per_seq + t, 0))
    return pl.pallas_call(
        _gla_kernel,
        out_shape=jax.ShapeDtypeStruct((ntok, GLA_DV), BF16),
        grid=(batch, tiles_per_seq),
        in_specs=[tok(GLA_DK), tok(GLA_DK), tok(GLA_DV), tok(GLA_DK), tok(GLA_DV), _const_spec((1, GLA_HV))],
        out_specs=tok(GLA_DV),
        scratch_shapes=[pltpu.VMEM((GLA_HEADS, GLA_HK, GLA_HV), F32), pltpu.VMEM((TM, GLA_DV), F32)],
        compiler_params=pltpu.CompilerParams(dimension_semantics=("arbitrary", "arbitrary"),
                                             vmem_limit_bytes=VMEM_LIMIT),
        name="gla",
    )(q, k, v, g, sr, gn)


def _merge_kernel(tiles_per_seq, first_layer, ya_ref, pooled_ref, ga_ref, gb_ref, h0_ref, h1_ref, h2_ref, head_ref,
                  wa_ref, wp_ref, ps_ref, wb_ref, wo_ref, out_ref):
    i = pl.program_id(0)
    h = _load_resid((h0_ref, h1_ref, h2_ref), head_ref, first_layer, i % tiles_per_seq)
    y_a = jnp.dot(ya_ref[...], wa_ref[...], preferred_element_type=F32)
    parts = [jnp.dot(pooled_ref[:, gi * POOL_GDIM:(gi + 1) * POOL_GDIM], wp_ref[gi],
                     preferred_element_type=F32) for gi in range(len(POOL_WINDOWS))]
    yb_in = (jnp.concatenate(parts, axis=1) * ps_ref[...]).astype(BF16)
    y_b = jnp.dot(yb_in, wb_ref[...], preferred_element_type=F32)
    m = (ga_ref[...].astype(F32) * y_a + gb_ref[...].astype(F32) * y_b).astype(BF16)
    h_new = h + jnp.dot(m, wo_ref[...], preferred_element_type=F32)
    row = _row_in_seq(i, tiles_per_seq, h_new.shape)
    out_ref[...] = jnp.where(row >= PAD, h_new, 0.0)


def _merge(ya, pooled, ga, gb, h, head, wa, wp, ps, wb, wo, tiles_per_seq, layer, seq):
    ntok = ya.shape[0]
    first_layer = layer == 0
    tok = pl.BlockSpec((TM, D_MODEL), lambda i: (i, 0))
    sq = _layer_spec((D_MODEL, D_MODEL), layer)
    return pl.pallas_call(
        functools.partial(_merge_kernel, tiles_per_seq, first_layer),
        out_shape=jax.ShapeDtypeStruct((ntok, D_MODEL), F32),
        grid=(ntok // TM,),
        in_specs=[tok, tok, tok, tok] + _resid_specs(first_layer, tiles_per_seq, seq) + [
            _const_spec((LEAD, D_MODEL)), sq,
            _layer_spec((len(POOL_WINDOWS), POOL_GDIM, POOL_GDIM), layer), _const_spec((1, D_MODEL)), sq, sq],
        out_specs=tok,
        compiler_params=pltpu.CompilerParams(dimension_semantics=("arbitrary",),
                                             vmem_limit_bytes=VMEM_LIMIT),
        name="merge",
    )(ya, pooled, ga, gb, h, h, h, head, wa, wp, ps, wb, wo)


def _ffn_kernel(tiles_per_seq, seq, last_layer, h_ref, halo_ref, g2_ref, wup_ref, cw_ref, cb_ref, wdn_ref, gf_ref,
                out_ref, slab_ref, *scratch):
    i = pl.program_id(0)
    tm = h_ref.shape[0]
    n_rows = tm + HALO
    n_vrow = n_rows // SUBLANES
    n_slab = D_MODEL // LANES
    n_chunks = D_FF // FF_CHUNK

    for c in range(n_slab):
        lanes = slice(c * LANES, (c + 1) * LANES)
        slab_ref[c, 0:tm, :] = h_ref[:, lanes]
        slab_ref[c, tm:n_rows, :] = halo_ref[:, lanes]
    h = jnp.concatenate(
        [jnp.concatenate([slab_ref[c, pl.ds(j, SUBLANES, stride=n_vrow), :] for c in range(n_slab)], axis=1)
         for j in range(n_vrow)], axis=0)
    hn = _rms(h, g2_ref[...]).astype(BF16)

    def pair(ref, j, rows=slice(None)):
        return jnp.concatenate([ref[rows, j * FF_CHUNK:(j + 1) * FF_CHUNK],
                                ref[rows, D_FF + j * FF_CHUNK:D_FF + (j + 1) * FF_CHUNK]], axis=1)

    def up_proj(j):
        return jnp.dot(hn, pair(wup_ref, j), preferred_element_type=F32)

    def earlier(x, k):
        cut = (n_vrow - k) * SUBLANES
        wrapped = [pltpu.roll(x[cut + r * SUBLANES:cut + (r + 1) * SUBLANES], 1, axis=0) for r in range(k)]
        return jnp.concatenate(wrapped + [x[:cut]], axis=0)

    def activation(up, j):
        c = pair(cb_ref, j) + pair(cw_ref, j, slice(CONV_W - 1, CONV_W)) * up
        for k in range(1, CONV_W):
            tap = CONV_W - 1 - k
            c = c + pair(cw_ref, j, slice(tap, tap + 1)) * earlier(up, k)
        a, bv = c[:, :FF_CHUNK], c[:, FF_CHUNK:]
        return (a * _sigmoid(a) * bv).astype(BF16)

    def down_proj(acts, j_end):
        k1 = j_end * FF_CHUNK
        lhs = acts[0] if len(acts) == 1 else jnp.concatenate(acts, axis=1)
        return jnp.dot(lhs, wdn_ref[k1 - lhs.shape[1]:k1, :], preferred_element_type=F32)

    group_ends = []
    for size in FF_GROUPS:
        group_ends.append((group_ends[-1] if group_ends else 0) + size)
    assert group_ends[-1] == n_chunks

    acc = h
    up = up_proj(0)
    acts, pending = [], None
    for j in range(n_chunks):
        up_next = up_proj(j + 1) if j + 1 < n_chunks else None
        if pending is not None:
            acc = acc + down_proj(*pending)
            pending = None
        acts.append(activation(up, j))
        up = up_next
        if j + 1 in group_ends:
            pending, acts = (acts, j + 1), []
    acc = acc + down_proj(*pending)

    res = _rms(acc, gf_ref[...]) if last_layer else acc
    for j in range(n_vrow):
        for c in range(n_slab):
            slab_ref[c, pl.ds(j, SUBLANES, stride=n_vrow), :] = res[j * SUBLANES:(j + 1) * SUBLANES,
                                                                     c * LANES:(c + 1) * LANES]

    if not last_layer:
        row = _row_in_seq(i, tiles_per_seq, (tm, LANES))
        for c in range(n_slab):
            out_ref[:, c * LANES:(c + 1) * LANES] = jnp.where(row >= PAD, slab_ref[c, 0:tm, :], 0.0)
        return

    obuf_ref, sem_ref = scratch
    n_steps = pl.num_programs(0)

    def tile_copy(step, head_tile):
        slot, b, t = step % 2, step // tiles_per_seq, step % tiles_per_seq
        if head_tile:
            src = obuf_ref.at[slot, pl.ds(LEAD, tm - LEAD)]
            dst = out_ref.at[pl.ds(b * seq, tm - LEAD)]
        else:
            src = obuf_ref.at[slot]
            dst = out_ref.at[pl.ds(b * seq + t * tm - LEAD, tm)]
        return pltpu.make_async_copy(src, dst, sem_ref.at[slot])

    def for_tile(step, action):
        head_tile = step % tiles_per_seq == 0

        @pl.when(head_tile)
        def _():
            action(tile_copy(step, True))

        @pl.when(jnp.logical_not(head_tile))
        def _():
            action(tile_copy(step, False))

    @pl.when(i >= 2)
    def _():
        for_tile(i - 2, lambda cp: cp.wait())

    for c in range(n_slab):
        obuf_ref[i % 2, :, c * LANES:(c + 1) * LANES] = slab_ref[c, 0:tm, :]
    for_tile(i, lambda cp: cp.start())

    @pl.when(i == n_steps - 1)
    def _():
        @pl.when(i >= 1)
        def _():
            for_tile(i - 1, lambda cp: cp.wait())
        for_tile(i, lambda cp: cp.wait())


def _ffn(h, g2, wup, cw, cb, wdn, gf, tiles_per_seq, seq, layer, last_layer):
    ntok = h.shape[0]
    tok = pl.BlockSpec((TM, D_MODEL), lambda i: (i, 0))
    halo = pl.BlockSpec((HALO, D_MODEL), lambda i: (jnp.maximum(i * (TM // HALO) - 1, 0), 0))
    slab = pltpu.VMEM((D_MODEL // LANES, TM + HALO, LANES), F32)
    if last_layer:
        out_rows = ntok // (LEAD + seq) * seq
        out_spec = pl.BlockSpec(memory_space=pl.ANY)
        scratch = [slab, pltpu.VMEM((2, TM, D_MODEL), F32), pltpu.SemaphoreType.DMA((2,))]
    else:
        out_rows, out_spec, scratch = ntok, tok, [slab]
    return pl.pallas_call(
        functools.partial(_ffn_kernel, tiles_per_seq, seq, last_layer),
        out_shape=jax.ShapeDtypeStruct((out_rows, D_MODEL), F32),
        grid=(ntok // TM,),
        in_specs=[tok, halo, _const_spec((1, D_MODEL)), _layer_spec((D_MODEL, 2 * D_FF), layer),
                  _const_spec((CONV_W, 2 * D_FF)), _const_spec((1, 2 * D_FF)), _layer_spec((D_FF, D_MODEL), layer),
                  _const_spec((1, D_MODEL))],
        out_specs=out_spec,
        scratch_shapes=scratch,
        compiler_params=pltpu.CompilerParams(dimension_semantics=("arbitrary",),
                                             vmem_limit_bytes=VMEM_LIMIT),
        name="ffn",
    )(h, h, g2, wup, cw, cb, wdn, gf)


def _regroup_kernel(w_ref, out_ref):
    x = w_ref[...]
    glr0 = _R0
    out_ref[:, :glr0] = x[:, :glr0].astype(BF16)
    out_ref[:, glr0:_GLR0] = x[:, glr0 + GLA_RANK:].astype(BF16)
    tail = jnp.concatenate([x[:, glr0:glr0 + GLA_RANK], jnp.zeros((x.shape[0], LANES - GLA_RANK), x.dtype)], axis=1)
    out_ref[:, _GLR0:] = tail.astype(BF16)


def _regroup_inproj(w_in):
    depth, d, cols = w_in.shape
    rows = 256
    return pl.pallas_call(
        _regroup_kernel,
        out_shape=jax.ShapeDtypeStruct((depth, d, IN_COLS), BF16),
        grid=(depth, d // rows),
        in_specs=[pl.BlockSpec((None, rows, cols), lambda l, i: (l, i, 0))],
        out_specs=pl.BlockSpec((None, rows, IN_COLS), lambda l, i: (l, i, 0)),
        compiler_params=pltpu.CompilerParams(dimension_semantics=("arbitrary", "arbitrary")),
        name="regroup_inproj",
    )(w_in)


def kernel(x, meta_tokens, norm1_g, w_in, w_gk, b_gk, gla_norm_g, w_a, w_pool_grp, pool_scale,
           w_b, b_gates, w_o, norm2_g, w_up, conv_w, conv_b, w_down, final_norm_g):
    batch, seq, d = x.shape
    depth = w_in.shape[0]
    lp = LEAD + seq
    assert d == D_MODEL and lp % TM == 0 and TM % LEAD == 0 and LEAD % CHUNK == 0 and seq % LEAD == 0
    tiles_per_seq = lp // TM
    ntok = batch * lp

    head = jnp.concatenate([jnp.zeros((PAD, d), F32), meta_tokens.astype(F32)], axis=0)
    h = x.astype(F32).reshape(batch * seq, d)

    assert w_in.shape[2] == _GLR0 + GLA_RANK and _R0 == 2 * GLA_DK + GLA_DV
    w_r = _regroup_inproj(w_in)
    wgk_pad = jnp.concatenate(
        [w_gk.astype(BF16), jnp.zeros((depth, LANES - GLA_RANK, GLA_DK), BF16)], axis=1)
    w_a16, w_b16, w_o16, w_pool16 = (w.astype(BF16) for w in (w_a, w_b, w_o, w_pool_grp))
    w_up16, w_down16 = w_up.astype(BF16), w_down.astype(BF16)

    for l in range(depth):
        q, k, v, g, sr, pooled, ga, gb = _inproj(
            h, head, norm1_g[l][None, :], w_r, wgk_pad, b_gk[l][None, :], b_gates[l][None, :],
            ntok, tiles_per_seq, l, seq)
        ya = _gla(q, k, v, g, sr, gla_norm_g[l][None, :], batch, tiles_per_seq)
        h = _merge(ya, pooled, ga, gb, h, head, w_a16, w_pool16, pool_scale[l][None, :], w_b16, w_o16,
                   tiles_per_seq, l, seq)
        h = _ffn(h, norm2_g[l][None, :], w_up16, conv_w[l], conv_b[l][None, :], w_down16,
                 final_norm_g[None, :], tiles_per_seq, seq, l, l == depth - 1)

    return h.reshape(batch, seq, d)
```

```python
import functools

import jax
import jax.numpy as jnp
from jax import lax
from jax.experimental import pallas as pl
from jax.experimental.pallas import tpu as pltpu

F32 = jnp.float32
BF16 = jnp.bfloat16

D_MODEL = 1024
N_META = 16
GLA_HEADS = 4
GLA_HK = 128
GLA_HV = 256
GLA_DK = GLA_HEADS * GLA_HK
GLA_DV = GLA_HEADS * GLA_HV
GLA_RANK = 16
GLA_TAU = 16.0
CHUNK = 64
POOL_WINDOWS = (2, 4, 8, 16)
POOL_GDIM = 256
D_FF = 2816
CONV_W = 3
EPS = 1e-6

PAD = 240
HALO = 16
TM = 768
LEAD = PAD + N_META
PARTS = TM // LEAD
INPROJ_COLS = 1024
FF_CHUNK = 256
FF_GROUPS = (4, 4, 2, 1)
LANES = 128
SUBLANES = 8
VMEM_LIMIT = 60 * 1024 * 1024

_Q0, _K0, _V0, _R0, _U0, _GA0, _GB0, _GLR0 = 0, 512, 1024, 2048, 3072, 4096, 5120, 6144
IN_COLS = _GLR0 + LANES


def _rms(x, g):
    ms = jnp.mean(x * x, axis=-1, keepdims=True)
    return x * lax.rsqrt(ms + EPS) * g


def _sigmoid(x):
    return 0.5 * jnp.tanh(0.5 * x) + 0.5


def _row_in_seq(tile_idx, tiles_per_seq, shape):
    base = (tile_idx % tiles_per_seq) * shape[0]
    return base + lax.broadcasted_iota(jnp.int32, shape, 0)


def _const_spec(shape):
    nd = len(shape)
    return pl.BlockSpec(shape, lambda *_: (0,) * nd, pipeline_mode=pl.Buffered(1))


def _layer_spec(shape, layer):
    nd = len(shape)
    return pl.BlockSpec((None,) + tuple(shape), lambda *_: (layer,) + (0,) * nd, pipeline_mode=pl.Buffered(1))


def _resid_specs(first_layer, tiles_per_seq, seq):
    def spec(p):
        if first_layer:
            def imap(i):
                b, t = i // tiles_per_seq, i % tiles_per_seq
                return (jnp.maximum(b * (seq // LEAD) + PARTS * t - 1 + p, 0), 0)
        else:
            imap = lambda i: (PARTS * i + p, 0)
        return pl.BlockSpec((LEAD, D_MODEL), imap)
    return [spec(p) for p in range(PARTS)]


def _load_resid(part_refs, head_ref, first_layer, tile_in_seq):
    parts = [r[...] for r in part_refs]
    if first_layer:
        parts[0] = jnp.where(tile_in_seq == 0, head_ref[...], parts[0])
    return jnp.concatenate(parts, axis=0)


def _inproj_kernel(tiles_per_seq, first_layer, h0_ref, h1_ref, h2_ref, head_ref, g1_ref, w_ref, wgk_ref, bgk_ref,
                   bg_ref, q_ref, k_ref, v_ref, g_ref, sr_ref, pooled_ref, ga_ref, gb_ref, carry_ref):
    i = pl.program_id(0)
    tm = TM

    @pl.when(i == 0)
    def _():
        carry_ref[...] = jnp.zeros_like(carry_ref)

    h = _load_resid((h0_ref, h1_ref, h2_ref), head_ref, first_layer, i % tiles_per_seq)
    hn = (h * g1_ref[...]).astype(BF16)
    inv = lax.rsqrt(jnp.mean(h * h, axis=-1, keepdims=True) + EPS)

    half = INPROJ_COLS
    head_pos = (i % tiles_per_seq) * tm + lax.broadcasted_iota(jnp.int32, (HALO, POOL_GDIM), 0)
    stash = {}

    def proj(c0, c1):
        return lambda: jnp.dot(hn, w_ref[:, c0:c1], preferred_element_type=F32) * inv

    def pool_epilogue(first_group):
        def fn(u):
            c0 = first_group * POOL_GDIM
            ext = jnp.concatenate([carry_ref[:, c0:c0 + half], u], axis=0)
            carry_ref[:, c0:c0 + half] = u[tm - HALO:, :]
            for gl in range(half // POOL_GDIM):
                w = POOL_WINDOWS[first_group + gl]
                loc = slice(gl * POOL_GDIM, (gl + 1) * POOL_GDIM)
                s = ext[:, loc]
                sh = 1
                while sh < w:
                    s = s + pltpu.roll(s, sh, axis=0)
                    sh *= 2
                out_cols = slice(c0 + gl * POOL_GDIM, c0 + (gl + 1) * POOL_GDIM)
                mean = s[HALO:, :] * (1.0 / w)
                pooled_ref[:, out_cols] = (mean - u[:, loc]).astype(BF16)
                cnt = jnp.clip(head_pos + 1, 1, w).astype(F32)
                pooled_ref[PAD:PAD + HALO, out_cols] = (
                    s[HALO + PAD:2 * HALO + PAD, :] / cnt - u[PAD:PAD + HALO, loc]).astype(BF16)
        return fn

    def glr_epilogue(x):
        stash["glr"] = x.astype(BF16)

    def forget_gate_epilogue(z):
        z = z + bgk_ref[...]
        g_ref[...] = (jnp.minimum(z, 0.0) - jnp.log(1.0 + jnp.exp(-jnp.abs(z)))) * (1.0 / GLA_TAU)

    def gate_epilogue(out_ref, bias0, c0):
        def fn(x):
            out_ref[:, c0:c0 + half] = _sigmoid(x + bg_ref[:, bias0 + c0:bias0 + c0 + half]).astype(BF16)
        return fn

    def silu_epilogue(c0):
        def fn(x):
            sr_ref[:, c0:c0 + half] = (x * _sigmoid(x)).astype(BF16)
        return fn

    def cast_epilogue(out_ref, c0):
        def fn(x):
            out_ref[:, c0:c0 + half] = x.astype(BF16)
        return fn

    assert half == D_MODEL == GLA_DV == 2 * GLA_DK

    def qk_epilogue(x):
        q_ref[...] = (x[:, :GLA_DK] * GLA_HK ** -0.5).astype(BF16)
        k_ref[...] = x[:, GLA_DK:].astype(BF16)

    stages = [
        (proj(_GLR0, IN_COLS), glr_epilogue),
        (proj(_U0, _GA0), pool_epilogue(0)),
        (proj(_GA0, _GB0), gate_epilogue(ga_ref, 0, 0)),
        (lambda: jnp.dot(stash["glr"], wgk_ref[...], preferred_element_type=F32), forget_gate_epilogue),
        (proj(_GB0, _GLR0), gate_epilogue(gb_ref, D_MODEL, 0)),
        (proj(_R0, _U0), silu_epilogue(0)),
        (proj(_V0, _R0), cast_epilogue(v_ref, 0)),
        (proj(_Q0, _V0), qk_epilogue),
    ]

    res = stages[0][0]()
    for idx, (_, epilogue) in enumerate(stages):
        nxt = stages[idx + 1][0]() if idx + 1 < len(stages) else None
        epilogue(res)
        res = nxt


def _inproj(h, head, g1, w_r, wgk_pad, bgk, bgates, ntok, tiles_per_seq, layer, seq):
    first_layer = layer == 0
    tok = lambda n: pl.BlockSpec((TM, n), lambda i: (i, 0))
    out_shape = (
        jax.ShapeDtypeStruct((ntok, GLA_DK), BF16),
        jax.ShapeDtypeStruct((ntok, GLA_DK), BF16),
        jax.ShapeDtypeStruct((ntok, GLA_DV), BF16),
        jax.ShapeDtypeStruct((ntok, GLA_DK), F32),
        jax.ShapeDtypeStruct((ntok, GLA_DV), BF16),
        jax.ShapeDtypeStruct((ntok, D_MODEL), BF16),
        jax.ShapeDtypeStruct((ntok, D_MODEL), BF16),
        jax.ShapeDtypeStruct((ntok, D_MODEL), BF16),
    )
    return pl.pallas_call(
        functools.partial(_inproj_kernel, tiles_per_seq, first_layer),
        out_shape=out_shape,
        grid=(ntok // TM,),
        in_specs=_resid_specs(first_layer, tiles_per_seq, seq) + [
            _const_spec((LEAD, D_MODEL)), _const_spec((1, D_MODEL)), _layer_spec((D_MODEL, IN_COLS), layer),
            _layer_spec((LANES, GLA_DK), layer), _const_spec((1, GLA_DK)), _const_spec((1, 2 * D_MODEL))],
        out_specs=(tok(GLA_DK), tok(GLA_DK), tok(GLA_DV), tok(GLA_DK), tok(GLA_DV),
                   tok(D_MODEL), tok(D_MODEL), tok(D_MODEL)),
        scratch_shapes=[pltpu.VMEM((HALO, D_MODEL), F32)],
        compiler_params=pltpu.CompilerParams(dimension_semantics=("arbitrary",),
                                             vmem_limit_bytes=VMEM_LIMIT),
        name="inproj",
    )(h, h, h, head, g1, w_r, wgk_pad, bgk, bgates)


def _gla_kernel(q_ref, k_ref, v_ref, g_ref, sr_ref, gn_ref, y_ref, state_ref, o_ref):
    t = pl.program_id(1)
    tg = q_ref.shape[0]

    @pl.when(t == 0)
    def _():
        state_ref[...] = jnp.zeros_like(state_ref)

    rr = lax.broadcasted_iota(jnp.int32, (CHUNK, CHUNK), 0)
    cc = lax.broadcasted_iota(jnp.int32, (CHUNK, CHUNK), 1)
    causal = rr >= cc
    tril = causal.astype(BF16)

    n_chunks = tg // CHUNK
    rows = lambda c: slice(c * CHUNK, (c + 1) * CHUNK)
    kcols = lambda h: slice(h * GLA_HK, (h + 1) * GLA_HK)
    vcols = lambda h: slice(h * GLA_HV, (h + 1) * GLA_HV)

    g = g_ref[...]
    g_hi = g.astype(BF16)
    g_lo = (g - g_hi.astype(F32)).astype(BF16)
    b_chunks = [jnp.dot(tril, g_hi[rows(c)], preferred_element_type=F32)
                + jnp.dot(tril, g_lo[rows(c)], preferred_element_type=F32) for c in range(n_chunks)]

    q_dec, k_end, dec, att, d_state = {}, {}, {}, {}, {}
    state = [state_ref[h] for h in range(GLA_HEADS)]

    def score_stage(c):
        b = b_chunks[c]
        dec[c] = jnp.exp(b[CHUNK - 1:CHUNK, :])
        grow = jnp.exp(b)
        q_dec[c] = (q_ref[rows(c), :].astype(F32) * grow).astype(BF16)
        k_inv = k_ref[rows(c), :].astype(F32) * (1.0 / grow)
        k_end[c] = (k_inv * dec[c]).astype(BF16)
        k_inv = k_inv.astype(BF16)
        for h in range(GLA_HEADS):
            att[c, h] = lax.dot_general(q_dec[c][:, kcols(h)], k_inv[:, kcols(h)], (((1,), (1,)), ((), ())),
                                        preferred_element_type=F32)

    def increment_stage(c):
        for h in range(GLA_HEADS):
            att[c, h] = jnp.where(causal, att[c, h], 0.0).astype(BF16)
            d_state[c, h] = lax.dot_general(k_end[c][:, kcols(h)], v_ref[rows(c), vcols(h)],
                                            (((0,), (0,)), ((), ())), preferred_element_type=F32)

    def output_stage(c):
        for h in range(GLA_HEADS):
            o_ref[rows(c), vcols(h)] = (
                jnp.dot(att.pop((c, h)), v_ref[rows(c), vcols(h)], preferred_element_type=F32)
                + jnp.dot(q_dec[c][:, kcols(h)], state[h].astype(BF16), preferred_element_type=F32))
            dec_col = jnp.broadcast_to(dec[c][:, kcols(h)], (GLA_HK, GLA_HK)).T
            state[h] = jnp.concatenate([dec_col, dec_col], axis=1) * state[h] + d_state.pop((c, h))

    for step in range(n_chunks + 2):
        if step < n_chunks:
            score_stage(step)
        if 0 <= step - 1 < n_chunks:
            increment_stage(step - 1)
        if 0 <= step - 2 < n_chunks:
            output_stage(step - 2)
    for h in range(GLA_HEADS):
        state_ref[h] = state[h]

    for h in range(GLA_HEADS):
        vs = slice(h * GLA_HV, (h + 1) * GLA_HV)
        y_ref[:, vs] = (_rms(o_ref[:, vs], gn_ref[...]) * sr_ref[:, vs].astype(F32)).astype(BF16)


def _gla(q, k, v, g, sr, gn, batch, tiles_per_seq):
    ntok = q.shape[0]
    tok = lambda n: pl.BlockSpec((TM, n), lambda b, t: (b * tiles_per_seq + t, 0))
    return pl.pallas_call(
        _gla_kernel,
        out_shape=jax.ShapeDtypeStruct((ntok, GLA_DV), BF16),
        grid=(batch, tiles_per_seq),
        in_specs=[tok(GLA_DK), tok(GLA_DK), tok(GLA_DV), tok(GLA_DK), tok(GLA_DV), _const_spec((1, GLA_HV))],
        out_specs=tok(GLA_DV),
        scratch_shapes=[pltpu.VMEM((GLA_HEADS, GLA_HK, GLA_HV), F32), pltpu.VMEM((TM, GLA_DV), F32)],
        compiler_params=pltpu.CompilerParams(dimension_semantics=("arbitrary", "arbitrary"),
                                             vmem_limit_bytes=VMEM_LIMIT),
        name="gla",
    )(q, k, v, g, sr, gn)


def _merge_kernel(tiles_per_seq, first_layer, ya_ref, pooled_ref, ga_ref, gb_ref, h0_ref, h1_ref, h2_ref, head_ref,
                  wa_ref, wp_ref, ps_ref, wb_ref, wo_ref, out_ref):
    i = pl.program_id(0)
    h = _load_resid((h0_ref, h1_ref, h2_ref), head_ref, first_layer, i % tiles_per_seq)
    y_a = jnp.dot(ya_ref[...], wa_ref[...], preferred_element_type=F32)
    parts = [jnp.dot(pooled_ref[:, gi * POOL_GDIM:(gi + 1) * POOL_GDIM], wp_ref[gi],
                     preferred_element_type=F32) for gi in range(len(POOL_WINDOWS))]
    yb_in = (jnp.concatenate(parts, axis=1) * ps_ref[...]).astype(BF16)
    y_b = jnp.dot(yb_in, wb_ref[...], preferred_element_type=F32)
    m = (ga_ref[...].astype(F32) * y_a + gb_ref[...].astype(F32) * y_b).astype(BF16)
    h_new = h + jnp.dot(m, wo_ref[...], preferred_element_type=F32)
    row = _row_in_seq(i, tiles_per_seq, h_new.shape)
    out_ref[...] = jnp.where(row >= PAD, h_new, 0.0)


def _merge(ya, pooled, ga, gb, h, head, wa, wp, ps, wb, wo, tiles_per_seq, layer, seq):
    ntok = ya.shape[0]
    first_layer = layer == 0
    tok = pl.BlockSpec((TM, D_MODEL), lambda i: (i, 0))
    sq = _layer_spec((D_MODEL, D_MODEL), layer)
    return pl.pallas_call(
        functools.partial(_merge_kernel, tiles_per_seq, first_layer),
        out_shape=jax.ShapeDtypeStruct((ntok, D_MODEL), F32),
        grid=(ntok // TM,),
        in_specs=[tok, tok, tok, tok] + _resid_specs(first_layer, tiles_per_seq, seq) + [
            _const_spec((LEAD, D_MODEL)), sq,
            _layer_spec((len(POOL_WINDOWS), POOL_GDIM, POOL_GDIM), layer), _const_spec((1, D_MODEL)), sq, sq],
        out_specs=tok,
        compiler_params=pltpu.CompilerParams(dimension_semantics=("arbitrary",),
                                             vmem_limit_bytes=VMEM_LIMIT),
        name="merge",
    )(ya, pooled, ga, gb, h, h, h, head, wa, wp, ps, wb, wo)


def _ffn_kernel(tiles_per_seq, seq, last_layer, h_ref, halo_ref, g2_ref, wup_ref, cw_ref, cb_ref, wdn_ref, gf_ref,
                out_ref, slab_ref, *scratch):
    i = pl.program_id(0)
    tm = h_ref.shape[0]
    n_rows = tm + HALO
    n_vrow = n_rows // SUBLANES
    n_slab = D_MODEL // LANES
    n_chunks = D_FF // FF_CHUNK

    for c in range(n_slab):
        lanes = slice(c * LANES, (c + 1) * LANES)
        slab_ref[c, 0:tm, :] = h_ref[:, lanes]
        slab_ref[c, tm:n_rows, :] = halo_ref[:, lanes]
    h = jnp.concatenate(
        [jnp.concatenate([slab_ref[c, pl.ds(j, SUBLANES, stride=n_vrow), :] for c in range(n_slab)], axis=1)
         for j in range(n_vrow)], axis=0)
    hg = h * g2_ref[...]
    hn_first = hg.astype(BF16)
    inv = lax.rsqrt(jnp.mean(h * h, axis=-1, keepdims=True) + EPS)
    hn = (hg * inv).astype(BF16)

    def pair(ref, j, rows=slice(None)):
        return jnp.concatenate([ref[rows, j * FF_CHUNK:(j + 1) * FF_CHUNK],
                                ref[rows, D_FF + j * FF_CHUNK:D_FF + (j + 1) * FF_CHUNK]], axis=1)

    def up_proj(j):
        if j == 0:
            return jnp.dot(hn_first, pair(wup_ref, j), preferred_element_type=F32) * inv
        return jnp.dot(hn, pair(wup_ref, j), preferred_element_type=F32)

    def earlier(x, k):
        cut = (n_vrow - k) * SUBLANES
        wrapped = [pltpu.roll(x[cut + r * SUBLANES:cut + (r + 1) * SUBLANES], 1, axis=0) for r in range(k)]
        return jnp.concatenate(wrapped + [x[:cut]], axis=0)

    def activation(up, j):
        c = pair(cb_ref, j) + pair(cw_ref, j, slice(CONV_W - 1, CONV_W)) * up
        for k in range(1, CONV_W):
            tap = CONV_W - 1 - k
            c = c + pair(cw_ref, j, slice(tap, tap + 1)) * earlier(up, k)
        a, bv = c[:, :FF_CHUNK], c[:, FF_CHUNK:]
        return (a * _sigmoid(a) * bv).astype(BF16)

    def down_proj(acts, j_end):
        k1 = j_end * FF_CHUNK
        lhs = acts[0] if len(acts) == 1 else jnp.concatenate(acts, axis=1)
        return jnp.dot(lhs, wdn_ref[k1 - lhs.shape[1]:k1, :], preferred_element_type=F32)

    group_ends = []
    for size in FF_GROUPS:
        group_ends.append((group_ends[-1] if group_ends else 0) + size)
    assert group_ends[-1] == n_chunks

    acc = h
    up = up_proj(0)
    acts, pending = [], None
    for j in range(n_chunks):
        up_next = up_proj(j + 1) if j + 1 < n_chunks else None
        if pending is not None:
            acc = acc + down_proj(*pending)
            pending = None
        acts.append(activation(up, j))
        up = up_next
        if j + 1 in group_ends:
            pending, acts = (acts, j + 1), []
    acc = acc + down_proj(*pending)

    res = _rms(acc, gf_ref[...]) if last_layer else acc
    for j in range(n_vrow):
        for c in range(n_slab):
            slab_ref[c, pl.ds(j, SUBLANES, stride=n_vrow), :] = res[j * SUBLANES:(j + 1) * SUBLANES,
                                                                     c * LANES:(c + 1) * LANES]

    if not last_layer:
        row = _row_in_seq(i, tiles_per_seq, (tm, LANES))
        for c in range(n_slab):
            out_ref[:, c * LANES:(c + 1) * LANES] = jnp.where(row >= PAD, slab_ref[c, 0:tm, :], 0.0)
        return

    obuf_ref, sem_ref = scratch
    n_steps = pl.num_programs(0)

    def tile_copy(step, head_tile):
        slot, b, t = step % 2, step // tiles_per_seq, step % tiles_per_seq
        if head_tile:
            src = obuf_ref.at[slot, pl.ds(LEAD, tm - LEAD)]
            dst = out_ref.at[pl.ds(b * seq, tm - LEAD)]
        else:
            src = obuf_ref.at[slot]
            dst = out_ref.at[pl.ds(b * seq + t * tm - LEAD, tm)]
        return pltpu.make_async_copy(src, dst, sem_ref.at[slot])

    def for_tile(step, action):
        head_tile = step % tiles_per_seq == 0

        @pl.when(head_tile)
        def _():
            action(tile_copy(step, True))

        @pl.when(jnp.logical_not(head_tile))
        def _():
            action(tile_copy(step, False))

    @pl.when(i >= 2)
    def _():
        for_tile(i - 2, lambda cp: cp.wait())

    for c in range(n_slab):
        obuf_ref[i % 2, :, c * LANES:(c + 1) * LANES] = slab_ref[c, 0:tm, :]
    for_tile(i, lambda cp: cp.start())

    @pl.when(i == n_steps - 1)
    def _():
        @pl.when(i >= 1)
        def _():
            for_tile(i - 1, lambda cp: cp.wait())
        for_tile(i, lambda cp: cp.wait())


def _ffn(h, g2, wup, cw, cb, wdn, gf, tiles_per_seq, seq, layer, last_layer):
    ntok = h.shape[0]
    tok = pl.BlockSpec((TM, D_MODEL), lambda i: (i, 0))
    halo = pl.BlockSpec((HALO, D_MODEL), lambda i: (jnp.maximum(i * (TM // HALO) - 1, 0), 0))
    slab = pltpu.VMEM((D_MODEL // LANES, TM + HALO, LANES), F32)
    if last_layer:
        out_rows = ntok // (LEAD + seq) * seq
        out_spec = pl.BlockSpec(memory_space=pl.ANY)
        scratch = [slab, pltpu.VMEM((2, TM, D_MODEL), F32), pltpu.SemaphoreType.DMA((2,))]
    else:
        out_rows, out_spec, scratch = ntok, tok, [slab]
    return pl.pallas_call(
        functools.partial(_ffn_kernel, tiles_per_seq, seq, last_layer),
        out_shape=jax.ShapeDtypeStruct((out_rows, D_MODEL), F32),
        grid=(ntok // TM,),
        in_specs=[tok, halo, _const_spec((1, D_MODEL)), _layer_spec((D_MODEL, 2 * D_FF), layer),
                  _const_spec((CONV_W, 2 * D_FF)), _const_spec((1, 2 * D_FF)), _layer_spec((D_FF, D_MODEL), layer),
                  _const_spec((1, D_MODEL))],
        out_specs=out_spec,
        scratch_shapes=scratch,
        compiler_params=pltpu.CompilerParams(dimension_semantics=("arbitrary",),
                                             vmem_limit_bytes=VMEM_LIMIT),
        name="ffn",
    )(h, h, g2, wup, cw, cb, wdn, gf)


def _regroup_kernel(w_ref, out_ref):
    x = w_ref[...]
    glr0 = _R0
    out_ref[:, :glr0] = x[:, :glr0].astype(BF16)
    out_ref[:, glr0:_GLR0] = x[:, glr0 + GLA_RANK:].astype(BF16)
    tail = jnp.concatenate([x[:, glr0:glr0 + GLA_RANK], jnp.zeros((x.shape[0], LANES - GLA_RANK), x.dtype)], axis=1)
    out_ref[:, _GLR0:] = tail.astype(BF16)


def _regroup_inproj(w_in):
    depth, d, cols = w_in.shape
    rows = 256
    return pl.pallas_call(
        _regroup_kernel,
        out_shape=jax.ShapeDtypeStruct((depth, d, IN_COLS), BF16),
        grid=(depth, d // rows),
        in_specs=[pl.BlockSpec((None, rows, cols), lambda l, i: (l, i, 0))],
        out_specs=pl.BlockSpec((None, rows, IN_COLS), lambda l, i: (l, i, 0)),
        compiler_params=pltpu.CompilerParams(dimension_semantics=("arbitrary", "arbitrary")),
        name="regroup_inproj",
    )(w_in)


def kernel(x, meta_tokens, norm1_g, w_in, w_gk, b_gk, gla_norm_g, w_a, w_pool_grp, pool_scale,
           w_b, b_gates, w_o, norm2_g, w_up, conv_w, conv_b, w_down, final_norm_g):
    batch, seq, d = x.shape
    depth = w_in.shape[0]
    lp = LEAD + seq
    assert d == D_MODEL and lp % TM == 0 and TM % LEAD == 0 and LEAD % CHUNK == 0 and seq % LEAD == 0
    tiles_per_seq = lp // TM
    ntok = batch * lp

    head = jnp.concatenate([jnp.zeros((PAD, d), F32), meta_tokens.astype(F32)], axis=0)
    h = x.astype(F32).reshape(batch * seq, d)

    assert w_in.shape[2] == _GLR0 + GLA_RANK and _R0 == 2 * GLA_DK + GLA_DV
    w_r = _regroup_inproj(w_in)
    wgk_pad = jnp.concatenate(
        [w_gk.astype(BF16), jnp.zeros((depth, LANES - GLA_RANK, GLA_DK), BF16)], axis=1)
    w_a16, w_b16, w_o16, w_pool16 = (w.astype(BF16) for w in (w_a, w_b, w_o, w_pool_grp))
    w_up16, w_down16 = w_up.astype(BF16), w_down.astype(BF16)

    for l in range(depth):
        q, k, v, g, sr, pooled, ga, gb = _inproj(
            h, head, norm1_g[l][None, :], w_r, wgk_pad, b_gk[l][None, :], b_gates[l][None, :],
            ntok, tiles_per_seq, l, seq)
        ya = _gla(q, k, v, g, sr, gla_norm_g[l][None, :], batch, tiles_per_seq)
        h = _merge(ya, pooled, ga, gb, h, head, w_a16, w_pool16, pool_scale[l][None, :], w_b16, w_o16,
                   tiles_per_seq, l, seq)
        h = _ffn(h, norm2_g[l][None, :], w_up16, conv_w[l], conv_b[l][None, :], w_down16,
                 final_norm_g[None, :], tiles_per_seq, seq, l, l == depth - 1)

    return h.reshape(batch, seq, d)
```

```python
import functools

import jax
import jax.numpy as jnp
from jax import lax
from jax.experimental import pallas as pl
from jax.experimental.pallas import tpu as pltpu

F32 = jnp.float32
BF16 = jnp.bfloat16

D_MODEL = 1024
N_META = 16
GLA_HEADS = 4
GLA_HK = 128
GLA_HV = 256
GLA_DK = GLA_HEADS * GLA_HK
GLA_DV = GLA_HEADS * GLA_HV
GLA_RANK = 16
GLA_TAU = 16.0
CHUNK = 64
POOL_WINDOWS = (2, 4, 8, 16)
POOL_GDIM = 256
D_FF = 2816
CONV_W = 3
EPS = 1e-6

PAD = 240
HALO = 16
TM = 768
LEAD = PAD + N_META
PARTS = TM // LEAD
INPROJ_COLS = 1024
FF_CHUNK = 256
FF_GROUPS = (4, 4, 2, 1)
LANES = 128
SUBLANES = 8
VMEM_LIMIT = 60 * 1024 * 1024

_Q0, _K0, _V0, _R0, _U0, _GA0, _GB0, _GLR0 = 0, 512, 1024, 2048, 3072, 4096, 5120, 6144
IN_COLS = _GLR0 + LANES


def _rms(x, g):
    ms = jnp.mean(x * x, axis=-1, keepdims=True)
    return x * lax.rsqrt(ms + EPS) * g


def _sigmoid(x):
    return 0.5 * jnp.tanh(0.5 * x) + 0.5


def _row_in_seq(tile_idx, tiles_per_seq, shape):
    base = (tile_idx % tiles_per_seq) * shape[0]
    return base + lax.broadcasted_iota(jnp.int32, shape, 0)


def _const_spec(shape):
    nd = len(shape)
    return pl.BlockSpec(shape, lambda *_: (0,) * nd, pipeline_mode=pl.Buffered(1))


def _layer_spec(shape, layer):
    nd = len(shape)
    return pl.BlockSpec((None,) + tuple(shape), lambda *_: (layer,) + (0,) * nd, pipeline_mode=pl.Buffered(1))


def _resid_specs(first_layer, tiles_per_seq, seq):
    def spec(p):
        if first_layer:
            def imap(i):
                b, t = i // tiles_per_seq, i % tiles_per_seq
                return (jnp.maximum(b * (seq // LEAD) + PARTS * t - 1 + p, 0), 0)
        else:
            imap = lambda i: (PARTS * i + p, 0)
        return pl.BlockSpec((LEAD, D_MODEL), imap)
    return [spec(p) for p in range(PARTS)]


def _load_resid(part_refs, head_ref, first_layer, tile_in_seq):
    parts = [r[...] for r in part_refs]
    if first_layer:
        parts[0] = jnp.where(tile_in_seq == 0, head_ref[...], parts[0])
    return jnp.concatenate(parts, axis=0)


def _inproj_kernel(tiles_per_seq, first_layer, h0_ref, h1_ref, h2_ref, head_ref, g1_ref, w_ref, wgk_ref, bgk_ref,
                   bg_ref, q_ref, k_ref, v_ref, g_ref, sr_ref, pooled_ref, ga_ref, gb_ref, carry_ref):
    i = pl.program_id(0)
    tm = TM

    @pl.when(i == 0)
    def _():
        carry_ref[...] = jnp.zeros_like(carry_ref)

    h = _load_resid((h0_ref, h1_ref, h2_ref), head_ref, first_layer, i % tiles_per_seq)
    hn = (h * g1_ref[...]).astype(BF16)
    inv = lax.rsqrt(jnp.mean(h * h, axis=-1, keepdims=True) + EPS)

    half = INPROJ_COLS
    head_pos = (i % tiles_per_seq) * tm + lax.broadcasted_iota(jnp.int32, (HALO, POOL_GDIM), 0)
    stash = {}

    def proj(c0, c1):
        return lambda: jnp.dot(hn, w_ref[:, c0:c1], preferred_element_type=F32) * inv

    def pool_epilogue(first_group):
        def fn(u):
            c0 = first_group * POOL_GDIM
            ext = jnp.concatenate([carry_ref[:, c0:c0 + half], u], axis=0)
            carry_ref[:, c0:c0 + half] = u[tm - HALO:, :]
            for gl in range(half // POOL_GDIM):
                w = POOL_WINDOWS[first_group + gl]
                loc = slice(gl * POOL_GDIM, (gl + 1) * POOL_GDIM)
                s = ext[:, loc]
                sh = 1
                while sh < w:
                    s = s + pltpu.roll(s, sh, axis=0)
                    sh *= 2
                out_cols = slice(c0 + gl * POOL_GDIM, c0 + (gl + 1) * POOL_GDIM)
                mean = s[HALO:, :] * (1.0 / w)
                pooled_ref[:, out_cols] = (mean - u[:, loc]).astype(BF16)
                cnt = jnp.clip(head_pos + 1, 1, w).astype(F32)
                pooled_ref[PAD:PAD + HALO, out_cols] = (
                    s[HALO + PAD:2 * HALO + PAD, :] / cnt - u[PAD:PAD + HALO, loc]).astype(BF16)
        return fn

    def glr_epilogue(x):
        stash["glr"] = x.astype(BF16)

    def forget_gate_epilogue(z):
        z = z + bgk_ref[...]
        g_ref[...] = (jnp.minimum(z, 0.0) - jnp.log(1.0 + jnp.exp(-jnp.abs(z)))) * (1.0 / GLA_TAU)

    def gate_epilogue(out_ref, bias0, c0):
        def fn(x):
            out_ref[:, c0:c0 + half] = _sigmoid(x + bg_ref[:, bias0 + c0:bias0 + c0 + half]).astype(BF16)
        return fn

    def silu_epilogue(c0):
        def fn(x):
            sr_ref[:, c0:c0 + half] = (x * _sigmoid(x)).astype(BF16)
        return fn

    def cast_epilogue(out_ref, c0):
        def fn(x):
            out_ref[:, c0:c0 + half] = x.astype(BF16)
        return fn

    assert half == D_MODEL == GLA_DV == 2 * GLA_DK

    def qk_epilogue(x):
        q_ref[...] = (x[:, :GLA_DK] * GLA_HK ** -0.5).astype(BF16)
        k_ref[...] = x[:, GLA_DK:].astype(BF16)

    stages = [
        (proj(_GLR0, IN_COLS), glr_epilogue),
        (proj(_U0, _GA0), pool_epilogue(0)),
        (proj(_GA0, _GB0), gate_epilogue(ga_ref, 0, 0)),
        (lambda: jnp.dot(stash["glr"], wgk_ref[...], preferred_element_type=F32), forget_gate_epilogue),
        (proj(_GB0, _GLR0), gate_epilogue(gb_ref, D_MODEL, 0)),
        (proj(_R0, _U0), silu_epilogue(0)),
        (proj(_V0, _R0), cast_epilogue(v_ref, 0)),
        (proj(_Q0, _V0), qk_epilogue),
    ]

    res = stages[0][0]()
    for idx, (_, epilogue) in enumerate(stages):
        nxt = stages[idx + 1][0]() if idx + 1 < len(stages) else None
        epilogue(res)
        res = nxt


def _inproj(h, head, g1, w_r, wgk_pad, bgk, bgates, ntok, tiles_per_seq, layer, seq):
    first_layer = layer == 0
    tok = lambda n: pl.BlockSpec((TM, n), lambda i: (i, 0))
    out_shape = (
        jax.ShapeDtypeStruct((ntok, GLA_DK), BF16),
        jax.ShapeDtypeStruct((ntok, GLA_DK), BF16),
        jax.ShapeDtypeStruct((ntok, GLA_DV), BF16),
        jax.ShapeDtypeStruct((ntok, GLA_DK), F32),
        jax.ShapeDtypeStruct((ntok, GLA_DV), BF16),
        jax.ShapeDtypeStruct((ntok, D_MODEL), BF16),
        jax.ShapeDtypeStruct((ntok, D_MODEL), BF16),
        jax.ShapeDtypeStruct((ntok, D_MODEL), BF16),
    )
    return pl.pallas_call(
        functools.partial(_inproj_kernel, tiles_per_seq, first_layer),
        out_shape=out_shape,
        grid=(ntok // TM,),
        in_specs=_resid_specs(first_layer, tiles_per_seq, seq) + [
            _const_spec((LEAD, D_MODEL)), _const_spec((1, D_MODEL)), _layer_spec((D_MODEL, IN_COLS), layer),
            _layer_spec((LANES, GLA_DK), layer), _const_spec((1, GLA_DK)), _const_spec((1, 2 * D_MODEL))],
        out_specs=(tok(GLA_DK), tok(GLA_DK), tok(GLA_DV), tok(GLA_DK), tok(GLA_DV),
                   tok(D_MODEL), tok(D_MODEL), tok(D_MODEL)),
        scratch_shapes=[pltpu.VMEM((HALO, D_MODEL), F32)],
        compiler_params=pltpu.CompilerParams(dimension_semantics=("arbitrary",),
                                             vmem_limit_bytes=VMEM_LIMIT),
        name="inproj",
    )(h, h, h, head, g1, w_r, wgk_pad, bgk, bgates)


def _gla_kernel(q_ref, k_ref, v_ref, g_ref, sr_ref, gn_ref, y_ref, state_ref, o_ref):
    t = pl.program_id(1)
    tg = q_ref.shape[0]

    @pl.when(t == 0)
    def _():
        state_ref[...] = jnp.zeros_like(state_ref)

    rr = lax.broadcasted_iota(jnp.int32, (CHUNK, CHUNK), 0)
    cc = lax.broadcasted_iota(jnp.int32, (CHUNK, CHUNK), 1)
    causal = rr >= cc
    tril = causal.astype(BF16)

    n_chunks = tg // CHUNK
    rows = lambda c: slice(c * CHUNK, (c + 1) * CHUNK)
    kcols = lambda h: slice(h * GLA_HK, (h + 1) * GLA_HK)
    vcols = lambda h: slice(h * GLA_HV, (h + 1) * GLA_HV)

    g = g_ref[...]
    g_hi = g.astype(BF16)
    g_lo = (g - g_hi.astype(F32)).astype(BF16)
    b_chunks = [jnp.dot(tril, g_hi[rows(c)], preferred_element_type=F32)
                + jnp.dot(tril, g_lo[rows(c)], preferred_element_type=F32) for c in range(n_chunks)]

    q_dec, k_end, dec, att, d_state = {}, {}, {}, {}, {}
    state = [state_ref[h] for h in range(GLA_HEADS)]

    def score_stage(c):
        b = b_chunks[c]
        dec[c] = jnp.exp(b[CHUNK - 1:CHUNK, :])
        grow = jnp.exp(b)
        q_dec[c] = (q_ref[rows(c), :].astype(F32) * grow).astype(BF16)
        k_inv = k_ref[rows(c), :].astype(F32) * (1.0 / grow)
        k_end[c] = (k_inv * dec[c]).astype(BF16)
        k_inv = k_inv.astype(BF16)
        for h in range(GLA_HEADS):
            att[c, h] = lax.dot_general(q_dec[c][:, kcols(h)], k_inv[:, kcols(h)], (((1,), (1,)), ((), ())),
                                        preferred_element_type=F32)

    def increment_stage(c):
        for h in range(GLA_HEADS):
            att[c, h] = jnp.where(causal, att[c, h], 0.0).astype(BF16)
            d_state[c, h] = lax.dot_general(k_end[c][:, kcols(h)], v_ref[rows(c), vcols(h)],
                                            (((0,), (0,)), ((), ())), preferred_element_type=F32)

    def output_stage(c):
        for h in range(GLA_HEADS):
            o_ref[rows(c), vcols(h)] = (
                jnp.dot(att.pop((c, h)), v_ref[rows(c), vcols(h)], preferred_element_type=F32)
                + jnp.dot(q_dec[c][:, kcols(h)], state[h].astype(BF16), preferred_element_type=F32))
            dec_col = jnp.broadcast_to(dec[c][:, kcols(h)], (GLA_HK, GLA_HK)).T
            state[h] = jnp.concatenate([dec_col, dec_col], axis=1) * state[h] + d_state.pop((c, h))

    for step in range(n_chunks + 2):
        if step < n_chunks:
            score_stage(step)
        if 0 <= step - 1 < n_chunks:
            increment_stage(step - 1)
        if 0 <= step - 2 < n_chunks:
            output_stage(step - 2)
    for h in range(GLA_HEADS):
        state_ref[h] = state[h]

    for h in range(GLA_HEADS):
        vs = slice(h * GLA_HV, (h + 1) * GLA_HV)
        y_ref[:, vs] = (_rms(o_ref[:, vs], gn_ref[...]) * sr_ref[:, vs].astype(F32)).astype(BF16)


def _gla(q, k, v, g, sr, gn, batch, tiles_per_seq):
    ntok = q.shape[0]
    tok = lambda n: pl.BlockSpec((TM, n), lambda b, t: (b * tiles_per_seq + t, 0))
    return pl.pallas_call(
        _gla_kernel,
        out_shape=jax.ShapeDtypeStruct((ntok, GLA_DV), BF16),
        grid=(batch, tiles_per_seq),
        in_specs=[tok(GLA_DK), tok(GLA_DK), tok(GLA_DV), tok(GLA_DK), tok(GLA_DV), _const_spec((1, GLA_HV))],
        out_specs=tok(GLA_DV),
        scratch_shapes=[pltpu.VMEM((GLA_HEADS, GLA_HK, GLA_HV), F32), pltpu.VMEM((TM, GLA_DV), F32)],
        compiler_params=pltpu.CompilerParams(dimension_semantics=("arbitrary", "arbitrary"),
                                             vmem_limit_bytes=VMEM_LIMIT),
        name="gla",
    )(q, k, v, g, sr, gn)


def _merge_kernel(tiles_per_seq, first_layer, ya_ref, pooled_ref, ga_ref, gb_ref, h0_ref, h1_ref, h2_ref, head_ref,
                  wa_ref, wp_ref, ps_ref, wb_ref, wo_ref, out_ref):
    i = pl.program_id(0)
    h = _load_resid((h0_ref, h1_ref, h2_ref), head_ref, first_layer, i % tiles_per_seq)
    y_a = jnp.dot(ya_ref[...], wa_ref[...], preferred_element_type=F32)
    parts = [jnp.dot(pooled_ref[:, gi * POOL_GDIM:(gi + 1) * POOL_GDIM], wp_ref[gi],
                     preferred_element_type=F32) for gi in range(len(POOL_WINDOWS))]
    yb_in = (jnp.concatenate(parts, axis=1) * ps_ref[...]).astype(BF16)
    y_b = jnp.dot(yb_in, wb_ref[...], preferred_element_type=F32)
    m = (ga_ref[...].astype(F32) * y_a + gb_ref[...].astype(F32) * y_b).astype(BF16)
    h_new = h + jnp.dot(m, wo_ref[...], preferred_element_type=F32)
    row = _row_in_seq(i, tiles_per_seq, h_new.shape)
    out_ref[...] = jnp.where(row >= PAD, h_new, 0.0)


def _merge(ya, pooled, ga, gb, h, head, wa, wp, ps, wb, wo, tiles_per_seq, layer, seq):
    ntok = ya.shape[0]
    first_layer = layer == 0
    tok = pl.BlockSpec((TM, D_MODEL), lambda i: (i, 0))
    sq = _layer_spec((D_MODEL, D_MODEL), layer)
    return pl.pallas_call(
        functools.partial(_merge_kernel, tiles_per_seq, first_layer),
        out_shape=jax.ShapeDtypeStruct((ntok, D_MODEL), F32),
        grid=(ntok // TM,),
        in_specs=[tok, tok, tok, tok] + _resid_specs(first_layer, tiles_per_seq, seq) + [
            _const_spec((LEAD, D_MODEL)), sq,
            _layer_spec((len(POOL_WINDOWS), POOL_GDIM, POOL_GDIM), layer), _const_spec((1, D_MODEL)), sq, sq],
        out_specs=tok,
        compiler_params=pltpu.CompilerParams(dimension_semantics=("arbitrary",),
                                             vmem_limit_bytes=VMEM_LIMIT),
        name="merge",
    )(ya, pooled, ga, gb, h, h, h, head, wa, wp, ps, wb, wo)


def _ffn_kernel(tiles_per_seq, seq, last_layer, h_ref, halo_ref, g2_ref, wup_ref, cw_ref, cb_ref, wdn_ref, gf_ref,
                out_ref, slab_ref, *scratch):
    i = pl.program_id(0)
    tm = h_ref.shape[0]
    n_rows = tm + HALO
    n_vrow = n_rows // SUBLANES
    n_slab = D_MODEL // LANES
    n_chunks = D_FF // FF_CHUNK

    for c in range(n_slab):
        lanes = slice(c * LANES, (c + 1) * LANES)
        slab_ref[c, 0:tm, :] = h_ref[:, lanes]
        slab_ref[c, tm:n_rows, :] = halo_ref[:, lanes]
    h = jnp.concatenate(
        [jnp.concatenate([slab_ref[c, pl.ds(j, SUBLANES, stride=n_vrow), :] for c in range(n_slab)], axis=1)
         for j in range(n_vrow)], axis=0)
    hg = h * g2_ref[...]
    hn_first = hg.astype(BF16)
    inv = lax.rsqrt(jnp.mean(h * h, axis=-1, keepdims=True) + EPS)
    hn = (hg * inv).astype(BF16)

    def pair(ref, j, rows=slice(None)):
        return jnp.concatenate([ref[rows, j * FF_CHUNK:(j + 1) * FF_CHUNK],
                                ref[rows, D_FF + j * FF_CHUNK:D_FF + (j + 1) * FF_CHUNK]], axis=1)

    def up_proj(j):
        if j == 0:
            return jnp.dot(hn_first, pair(wup_ref, j), preferred_element_type=F32) * inv
        return jnp.dot(hn, pair(wup_ref, j), preferred_element_type=F32)

    def earlier(x, k):
        cut = (n_vrow - k) * SUBLANES
        wrapped = [pltpu.roll(x[cut + r * SUBLANES:cut + (r + 1) * SUBLANES], 1, axis=0) for r in range(k)]
        return jnp.concatenate(wrapped + [x[:cut]], axis=0)

    def activation(up, j):
        c = pair(cb_ref, j) + pair(cw_ref, j, slice(CONV_W - 1, CONV_W)) * up
        for k in range(1, CONV_W):
            tap = CONV_W - 1 - k
            c = c + pair(cw_ref, j, slice(tap, tap + 1)) * earlier(up, k)
        a, bv = c[:, :FF_CHUNK], c[:, FF_CHUNK:]
        return (a * _sigmoid(a) * bv).astype(BF16)

    def down_proj(acts, j_end):
        k1 = j_end * FF_CHUNK
        lhs = acts[0] if len(acts) == 1 else jnp.concatenate(acts, axis=1)
        return jnp.dot(lhs, wdn_ref[k1 - lhs.shape[1]:k1, :], preferred_element_type=F32)

    group_ends = []
    for size in FF_GROUPS:
        group_ends.append((group_ends[-1] if group_ends else 0) + size)
    assert group_ends[-1] == n_chunks

    acc = h
    up = up_proj(0)
    acts, pending = [], None
    for j in range(n_chunks):
        up_next = up_proj(j + 1) if j + 1 < n_chunks else None
        if pending is not None:
            acc = acc + down_proj(*pending)
            pending = None
        acts.append(activation(up, j))
        up = up_next
        if j + 1 in group_ends:
            pending, acts = (acts, j + 1), []
    acc = acc + down_proj(*pending)

    res = _rms(acc, gf_ref[...]) if last_layer else acc
    for j in range(n_vrow):
        for c in range(n_slab):
            slab_ref[c, pl.ds(j, SUBLANES, stride=n_vrow), :] = res[j * SUBLANES:(j + 1) * SUBLANES,
                                                                     c * LANES:(c + 1) * LANES]

    if not last_layer:
        row = _row_in_seq(i, tiles_per_seq, (tm, LANES))
        for c in range(n_slab):
            out_ref[:, c * LANES:(c + 1) * LANES] = jnp.where(row >= PAD, slab_ref[c, 0:tm, :], 0.0)
        return

    obuf_ref, sem_ref = scratch
    n_steps = pl.num_programs(0)

    def tile_copy(step, head_tile):
        slot, b, t = step % 2, step // tiles_per_seq, step % tiles_per_seq
        if head_tile:
            src = obuf_ref.at[slot, pl.ds(LEAD, tm - LEAD)]
            dst = out_ref.at[pl.ds(b * seq, tm - LEAD)]
        else:
            src = obuf_ref.at[slot]
            dst = out_ref.at[pl.ds(b * seq + t * tm - LEAD, tm)]
        return pltpu.make_async_copy(src, dst, sem_ref.at[slot])

    def for_tile(step, action):
        head_tile = step % tiles_per_seq == 0

        @pl.when(head_tile)
        def _():
            action(tile_copy(step, True))

        @pl.when(jnp.logical_not(head_tile))
        def _():
            action(tile_copy(step, False))

    @pl.when(i >= 2)
    def _():
        for_tile(i - 2, lambda cp: cp.wait())

    for c in range(n_slab):
        obuf_ref[i % 2, :, c * LANES:(c + 1) * LANES] = slab_ref[c, 0:tm, :]
    for_tile(i, lambda cp: cp.start())

    @pl.when(i == n_steps - 1)
    def _():
        @pl.when(i >= 1)
        def _():
            for_tile(i - 1, lambda cp: cp.wait())
        for_tile(i, lambda cp: cp.wait())


def _ffn(h, g2, wup, cw, cb, wdn, gf, tiles_per_seq, seq, layer, last_layer):
    ntok = h.shape[0]
    tok = pl.BlockSpec((TM, D_MODEL), lambda i: (i, 0))
    halo = pl.BlockSpec((HALO, D_MODEL), lambda i: (jnp.maximum(i * (TM // HALO) - 1, 0), 0))
    slab = pltpu.VMEM((D_MODEL // LANES, TM + HALO, LANES), F32)
    if last_layer:
        out_rows = ntok // (LEAD + seq) * seq
        out_spec = pl.BlockSpec(memory_space=pl.ANY)
        scratch = [slab, pltpu.VMEM((2, TM, D_MODEL), F32), pltpu.SemaphoreType.DMA((2,))]
    else:
        out_rows, out_spec, scratch = ntok, tok, [slab]
    return pl.pallas_call(
        functools.partial(_ffn_kernel, tiles_per_seq, seq, last_layer),
        out_shape=jax.ShapeDtypeStruct((out_rows, D_MODEL), F32),
        grid=(ntok // TM,),
        in_specs=[tok, halo, _const_spec((1, D_MODEL)), _layer_spec((D_MODEL, 2 * D_FF), layer),
                  _const_spec((CONV_W, 2 * D_FF)), _const_spec((1, 2 * D_FF)), _layer_spec((D_FF, D_MODEL), layer),
                  _const_spec((1, D_MODEL))],
        out_specs=out_spec,
        scratch_shapes=scratch,
        compiler_params=pltpu.CompilerParams(dimension_semantics=("arbitrary",),
                                             vmem_limit_bytes=VMEM_LIMIT),
        name="ffn",
    )(h, h, g2, wup, cw, cb, wdn, gf)


def _regroup_kernel(wt_ref, out_ref):
    def put(col0, row0):
        out_ref[:, col0:col0 + LANES] = wt_ref[row0:row0 + LANES, :].T.astype(BF16)

    for j in range(_R0 // LANES):
        put(j * LANES, j * LANES)
    for j in range((_GLR0 - _R0) // LANES):
        put(_R0 + j * LANES, _R0 + GLA_RANK + j * LANES)
    tail = wt_ref[_R0:_R0 + LANES, :].T
    lane = lax.broadcasted_iota(jnp.int32, tail.shape, 1)
    out_ref[:, _GLR0:] = jnp.where(lane < GLA_RANK, tail, 0.0).astype(BF16)


def _regroup_inproj(w_in):
    depth, d, cols = w_in.shape
    return pl.pallas_call(
        _regroup_kernel,
        out_shape=jax.ShapeDtypeStruct((depth, d, IN_COLS), BF16),
        grid=(depth,),
        in_specs=[pl.BlockSpec((None, cols, d), lambda l: (l, 0, 0), pipeline_mode=pl.Buffered(1))],
        out_specs=pl.BlockSpec((None, d, IN_COLS), lambda l: (l, 0, 0)),
        compiler_params=pltpu.CompilerParams(dimension_semantics=("arbitrary",), vmem_limit_bytes=VMEM_LIMIT),
        name="regroup_inproj",
    )(jnp.swapaxes(w_in, 1, 2))


def kernel(x, meta_tokens, norm1_g, w_in, w_gk, b_gk, gla_norm_g, w_a, w_pool_grp, pool_scale,
           w_b, b_gates, w_o, norm2_g, w_up, conv_w, conv_b, w_down, final_norm_g):
    batch, seq, d = x.shape
    depth = w_in.shape[0]
    lp = LEAD + seq
    assert d == D_MODEL and lp % TM == 0 and TM % LEAD == 0 and LEAD % CHUNK == 0 and seq % LEAD == 0
    tiles_per_seq = lp // TM
    ntok = batch * lp

    head = jnp.concatenate([jnp.zeros((PAD, d), F32), meta_tokens.astype(F32)], axis=0)
    h = x.astype(F32).reshape(batch * seq, d)

    assert w_in.shape[2] == _GLR0 + GLA_RANK and _R0 == 2 * GLA_DK + GLA_DV
    w_r = _regroup_inproj(w_in)
    wgk_pad = jnp.concatenate(
        [w_gk.astype(BF16), jnp.zeros((depth, LANES - GLA_RANK, GLA_DK), BF16)], axis=1)
    w_a16, w_b16, w_o16, w_pool16 = (w.astype(BF16) for w in (w_a, w_b, w_o, w_pool_grp))
    w_up16, w_down16 = w_up.astype(BF16), w_down.astype(BF16)

    for l in range(depth):
        q, k, v, g, sr, pooled, ga, gb = _inproj(
            h, head, norm1_g[l][None, :], w_r, wgk_pad, b_gk[l][None, :], b_gates[l][None, :],
            ntok, tiles_per_seq, l, seq)
        ya = _gla(q, k, v, g, sr, gla_norm_g[l][None, :], batch, tiles_per_seq)
        h = _merge(ya, pooled, ga, gb, h, head, w_a16, w_pool16, pool_scale[l][None, :], w_b16, w_o16,
                   tiles_per_seq, l, seq)
        h = _ffn(h, norm2_g[l][None, :], w_up16, conv_w[l], conv_b[l][None, :], w_down16,
                 final_norm_g[None, :], tiles_per_seq, seq, l, l == depth - 1)

    return h.reshape(batch, seq, d)
```

```python
import functools

import jax
import jax.numpy as jnp
from jax import lax
from jax.experimental import pallas as pl
from jax.experimental.pallas import tpu as pltpu

F32 = jnp.float32
BF16 = jnp.bfloat16

D_MODEL = 1024
N_META = 16
GLA_HEADS = 4
GLA_HK = 128
GLA_HV = 256
GLA_DK = GLA_HEADS * GLA_HK
GLA_DV = GLA_HEADS * GLA_HV
GLA_RANK = 16
GLA_TAU = 16.0
CHUNK = 64
POOL_WINDOWS = (2, 4, 8, 16)
POOL_GDIM = 256
D_FF = 2816
CONV_W = 3
EPS = 1e-6

PAD = 240
HALO = 16
TM = 768
LEAD = PAD + N_META
PARTS = TM // LEAD
INPROJ_COLS = 1024
FF_CHUNK = 256
FF_GROUPS = (4, 4, 2, 1)
LANES = 128
SUBLANES = 8
VMEM_LIMIT = 60 * 1024 * 1024

_Q0, _K0, _V0, _R0, _U0, _GA0, _GB0, _GLR0 = 0, 512, 1024, 2048, 3072, 4096, 5120, 6144
IN_COLS = _GLR0 + LANES


def _rms(x, g):
    ms = jnp.mean(x * x, axis=-1, keepdims=True)
    return x * lax.rsqrt(ms + EPS) * g


def _sigmoid(x):
    return 0.5 * jnp.tanh(0.5 * x) + 0.5


def _row_in_seq(tile_idx, tiles_per_seq, shape):
    base = (tile_idx % tiles_per_seq) * shape[0]
    return base + lax.broadcasted_iota(jnp.int32, shape, 0)


def _const_spec(shape):
    nd = len(shape)
    return pl.BlockSpec(shape, lambda *_: (0,) * nd, pipeline_mode=pl.Buffered(1))


def _layer_spec(shape, layer):
    nd = len(shape)
    return pl.BlockSpec((None,) + tuple(shape), lambda *_: (layer,) + (0,) * nd, pipeline_mode=pl.Buffered(1))


def _resid_specs(first_layer, tiles_per_seq, seq):
    def spec(p):
        if first_layer:
            def imap(i):
                b, t = i // tiles_per_seq, i % tiles_per_seq
                return (jnp.maximum(b * (seq // LEAD) + PARTS * t - 1 + p, 0), 0)
        else:
            imap = lambda i: (PARTS * i + p, 0)
        return pl.BlockSpec((LEAD, D_MODEL), imap)
    return [spec(p) for p in range(PARTS)]


def _load_resid(part_refs, head_ref, first_layer, tile_in_seq):
    parts = [r[...] for r in part_refs]
    if first_layer:
        parts[0] = jnp.where(tile_in_seq == 0, head_ref[...], parts[0])
    return jnp.concatenate(parts, axis=0)


def _inproj_kernel(tiles_per_seq, first_layer, h0_ref, h1_ref, h2_ref, head_ref, g1_ref, w_ref, wgk_ref, bgk_ref,
                   bg_ref, q_ref, k_ref, v_ref, g_ref, sr_ref, pooled_ref, ga_ref, gb_ref, carry_ref):
    i = pl.program_id(0)
    tm = TM

    @pl.when(i == 0)
    def _():
        carry_ref[...] = jnp.zeros_like(carry_ref)

    h = _load_resid((h0_ref, h1_ref, h2_ref), head_ref, first_layer, i % tiles_per_seq)
    hn = (h * g1_ref[...]).astype(BF16)
    inv = lax.rsqrt(jnp.mean(h * h, axis=-1, keepdims=True) + EPS)

    half = INPROJ_COLS
    head_pos = (i % tiles_per_seq) * tm + lax.broadcasted_iota(jnp.int32, (HALO, POOL_GDIM), 0)
    stash = {}

    def proj(c0, c1):
        return lambda: jnp.dot(hn, w_ref[:, c0:c1], preferred_element_type=F32) * inv

    def pool_epilogue(first_group):
        def fn(u):
            c0 = first_group * POOL_GDIM
            ext = jnp.concatenate([carry_ref[:, c0:c0 + half], u], axis=0)
            carry_ref[:, c0:c0 + half] = u[tm - HALO:, :]
            for gl in range(half // POOL_GDIM):
                w = POOL_WINDOWS[first_group + gl]
                loc = slice(gl * POOL_GDIM, (gl + 1) * POOL_GDIM)
                s = ext[:, loc]
                sh = 1
                while sh < w:
                    s = s + pltpu.roll(s, sh, axis=0)
                    sh *= 2
                out_cols = slice(c0 + gl * POOL_GDIM, c0 + (gl + 1) * POOL_GDIM)
                mean = s[HALO:, :] * (1.0 / w)
                pooled_ref[:, out_cols] = (mean - u[:, loc]).astype(BF16)
                cnt = jnp.clip(head_pos + 1, 1, w).astype(F32)
                pooled_ref[PAD:PAD + HALO, out_cols] = (
                    s[HALO + PAD:2 * HALO + PAD, :] / cnt - u[PAD:PAD + HALO, loc]).astype(BF16)
        return fn

    def glr_epilogue(x):
        stash["glr"] = x.astype(BF16)

    def forget_gate_epilogue(z):
        z = z + bgk_ref[...]
        g_ref[...] = (jnp.minimum(z, 0.0) - jnp.log(1.0 + jnp.exp(-jnp.abs(z)))) * (1.0 / GLA_TAU)

    def gate_epilogue(out_ref, bias0, c0):
        def fn(x):
            out_ref[:, c0:c0 + half] = _sigmoid(x + bg_ref[:, bias0 + c0:bias0 + c0 + half]).astype(BF16)
        return fn

    def silu_epilogue(c0):
        def fn(x):
            sr_ref[:, c0:c0 + half] = (x * _sigmoid(x)).astype(BF16)
        return fn

    def cast_epilogue(out_ref, c0):
        def fn(x):
            out_ref[:, c0:c0 + half] = x.astype(BF16)
        return fn

    assert half == D_MODEL == GLA_DV == 2 * GLA_DK

    def qk_epilogue(x):
        q_ref[...] = (x[:, :GLA_DK] * GLA_HK ** -0.5).astype(BF16)
        k_ref[...] = x[:, GLA_DK:].astype(BF16)

    stages = [
        (proj(_GLR0, IN_COLS), glr_epilogue),
        (proj(_U0, _GA0), pool_epilogue(0)),
        (proj(_GA0, _GB0), gate_epilogue(ga_ref, 0, 0)),
        (lambda: jnp.dot(stash["glr"], wgk_ref[...], preferred_element_type=F32), forget_gate_epilogue),
        (proj(_GB0, _GLR0), gate_epilogue(gb_ref, D_MODEL, 0)),
        (proj(_R0, _U0), silu_epilogue(0)),
        (proj(_V0, _R0), cast_epilogue(v_ref, 0)),
        (proj(_Q0, _V0), qk_epilogue),
    ]

    res = stages[0][0]()
    for idx, (_, epilogue) in enumerate(stages):
        nxt = stages[idx + 1][0]() if idx + 1 < len(stages) else None
        epilogue(res)
        res = nxt


def _inproj(h, head, g1, w_r, wgk_pad, bgk, bgates, ntok, tiles_per_seq, layer, seq):
    first_layer = layer == 0
    tok = lambda n: pl.BlockSpec((TM, n), lambda i: (i, 0))
    out_shape = (
        jax.ShapeDtypeStruct((ntok, GLA_DK), BF16),
        jax.ShapeDtypeStruct((ntok, GLA_DK), BF16),
        jax.ShapeDtypeStruct((ntok, GLA_DV), BF16),
        jax.ShapeDtypeStruct((ntok, GLA_DK), F32),
        jax.ShapeDtypeStruct((ntok, GLA_DV), BF16),
        jax.ShapeDtypeStruct((ntok, D_MODEL), BF16),
        jax.ShapeDtypeStruct((ntok, D_MODEL), BF16),
        jax.ShapeDtypeStruct((ntok, D_MODEL), BF16),
    )
    return pl.pallas_call(
        functools.partial(_inproj_kernel, tiles_per_seq, first_layer),
        out_shape=out_shape,
        grid=(ntok // TM,),
        in_specs=_resid_specs(first_layer, tiles_per_seq, seq) + [
            _const_spec((LEAD, D_MODEL)), _const_spec((1, D_MODEL)), _layer_spec((D_MODEL, IN_COLS), layer),
            _layer_spec((LANES, GLA_DK), layer), _const_spec((1, GLA_DK)), _const_spec((1, 2 * D_MODEL))],
        out_specs=(tok(GLA_DK), tok(GLA_DK), tok(GLA_DV), tok(GLA_DK), tok(GLA_DV),
                   tok(D_MODEL), tok(D_MODEL), tok(D_MODEL)),
        scratch_shapes=[pltpu.VMEM((HALO, D_MODEL), F32)],
        compiler_params=pltpu.CompilerParams(dimension_semantics=("arbitrary",),
                                             vmem_limit_bytes=VMEM_LIMIT),
        name="inproj",
    )(h, h, h, head, g1, w_r, wgk_pad, bgk, bgates)


def _gla_kernel(q_ref, k_ref, v_ref, g_ref, o_ref, state_ref):
    t = pl.program_id(1)
    tg = q_ref.shape[0]

    @pl.when(t == 0)
    def _():
        state_ref[...] = jnp.zeros_like(state_ref)

    rr = lax.broadcasted_iota(jnp.int32, (CHUNK, CHUNK), 0)
    cc = lax.broadcasted_iota(jnp.int32, (CHUNK, CHUNK), 1)
    causal = rr >= cc
    tril = causal.astype(BF16)

    n_chunks = tg // CHUNK
    rows = lambda c: slice(c * CHUNK, (c + 1) * CHUNK)
    kcols = lambda h: slice(h * GLA_HK, (h + 1) * GLA_HK)
    vcols = lambda h: slice(h * GLA_HV, (h + 1) * GLA_HV)

    g = g_ref[...]
    g_hi = g.astype(BF16)
    g_lo = (g - g_hi.astype(F32)).astype(BF16)
    b_chunks = [jnp.dot(tril, g_hi[rows(c)], preferred_element_type=F32)
                + jnp.dot(tril, g_lo[rows(c)], preferred_element_type=F32) for c in range(n_chunks)]

    q_dec, k_end, dec, att, d_state = {}, {}, {}, {}, {}
    state = [state_ref[h] for h in range(GLA_HEADS)]

    def score_stage(c):
        b = b_chunks[c]
        dec[c] = jnp.exp(b[CHUNK - 1:CHUNK, :])
        grow = jnp.exp(b)
        q_dec[c] = (q_ref[rows(c), :].astype(F32) * grow).astype(BF16)
        k_inv = k_ref[rows(c), :].astype(F32) * (1.0 / grow)
        k_end[c] = (k_inv * dec[c]).astype(BF16)
        k_inv = k_inv.astype(BF16)
        for h in range(GLA_HEADS):
            att[c, h] = lax.dot_general(q_dec[c][:, kcols(h)], k_inv[:, kcols(h)], (((1,), (1,)), ((), ())),
                                        preferred_element_type=F32)

    def increment_stage(c):
        for h in range(GLA_HEADS):
            att[c, h] = jnp.where(causal, att[c, h], 0.0).astype(BF16)
            d_state[c, h] = lax.dot_general(k_end[c][:, kcols(h)], v_ref[rows(c), vcols(h)],
                                            (((0,), (0,)), ((), ())), preferred_element_type=F32)

    def output_stage(c):
        for h in range(GLA_HEADS):
            o_ref[rows(c), vcols(h)] = (
                jnp.dot(att.pop((c, h)), v_ref[rows(c), vcols(h)], preferred_element_type=F32)
                + jnp.dot(q_dec[c][:, kcols(h)], state[h].astype(BF16), preferred_element_type=F32)).astype(BF16)
            dec_col = jnp.broadcast_to(dec[c][:, kcols(h)], (GLA_HK, GLA_HK)).T
            state[h] = jnp.concatenate([dec_col, dec_col], axis=1) * state[h] + d_state.pop((c, h))

    for step in range(n_chunks + 2):
        if step < n_chunks:
            score_stage(step)
        if 0 <= step - 1 < n_chunks:
            increment_stage(step - 1)
        if 0 <= step - 2 < n_chunks:
            output_stage(step - 2)
    for h in range(GLA_HEADS):
        state_ref[h] = state[h]


def _gla(q, k, v, g, batch, tiles_per_seq):
    ntok = q.shape[0]
    tok = lambda n: pl.BlockSpec((TM, n), lambda b, t: (b * tiles_per_seq + t, 0))
    return pl.pallas_call(
        _gla_kernel,
        out_shape=jax.ShapeDtypeStruct((ntok, GLA_DV), BF16),
        grid=(batch, tiles_per_seq),
        in_specs=[tok(GLA_DK), tok(GLA_DK), tok(GLA_DV), tok(GLA_DK)],
        out_specs=tok(GLA_DV),
        scratch_shapes=[pltpu.VMEM((GLA_HEADS, GLA_HK, GLA_HV), F32)],
        compiler_params=pltpu.CompilerParams(dimension_semantics=("arbitrary", "arbitrary"),
                                             vmem_limit_bytes=VMEM_LIMIT),
        name="gla",
    )(q, k, v, g)


def _merge_kernel(tiles_per_seq, first_layer, o_ref, sr_ref, gn_ref, pooled_ref, ga_ref, gb_ref, h0_ref, h1_ref, h2_ref,
                  head_ref, wa_ref, wp_ref, ps_ref, wb_ref, wo_ref, out_ref):
    i = pl.program_id(0)
    h = _load_resid((h0_ref, h1_ref, h2_ref), head_ref, first_layer, i % tiles_per_seq)
    parts = [jnp.dot(pooled_ref[:, gi * POOL_GDIM:(gi + 1) * POOL_GDIM], wp_ref[gi],
                     preferred_element_type=F32) for gi in range(len(POOL_WINDOWS))]
    yb_in = (jnp.concatenate(parts, axis=1) * ps_ref[...]).astype(BF16)
    y_b = jnp.dot(yb_in, wb_ref[...], preferred_element_type=F32)
    ya = jnp.concatenate(
        [(_rms(o_ref[:, hd * GLA_HV:(hd + 1) * GLA_HV].astype(F32), gn_ref[...])
          * sr_ref[:, hd * GLA_HV:(hd + 1) * GLA_HV].astype(F32)).astype(BF16) for hd in range(GLA_HEADS)], axis=1)
    y_a = jnp.dot(ya, wa_ref[...], preferred_element_type=F32)
    m = (ga_ref[...].astype(F32) * y_a + gb_ref[...].astype(F32) * y_b).astype(BF16)
    h_new = h + jnp.dot(m, wo_ref[...], preferred_element_type=F32)
    row = _row_in_seq(i, tiles_per_seq, h_new.shape)
    out_ref[...] = jnp.where(row >= PAD, h_new, 0.0)


def _merge(o, sr, gn, pooled, ga, gb, h, head, wa, wp, ps, wb, wo, tiles_per_seq, layer, seq):
    ntok = o.shape[0]
    first_layer = layer == 0
    tok = pl.BlockSpec((TM, D_MODEL), lambda i: (i, 0))
    sq = _layer_spec((D_MODEL, D_MODEL), layer)
    return pl.pallas_call(
        functools.partial(_merge_kernel, tiles_per_seq, first_layer),
        out_shape=jax.ShapeDtypeStruct((ntok, D_MODEL), F32),
        grid=(ntok // TM,),
        in_specs=[tok, tok, _const_spec((1, GLA_HV)), tok, tok, tok] + _resid_specs(first_layer, tiles_per_seq, seq) + [
            _const_spec((LEAD, D_MODEL)), sq,
            _layer_spec((len(POOL_WINDOWS), POOL_GDIM, POOL_GDIM), layer), _const_spec((1, D_MODEL)), sq, sq],
        out_specs=tok,
        compiler_params=pltpu.CompilerParams(dimension_semantics=("arbitrary",),
                                             vmem_limit_bytes=VMEM_LIMIT),
        name="merge",
    )(o, sr, gn, pooled, ga, gb, h, h, h, head, wa, wp, ps, wb, wo)


def _ffn_kernel(tiles_per_seq, seq, last_layer, h_ref, halo_ref, g2_ref, wup_ref, cw_ref, cb_ref, wdn_ref, gf_ref,
                out_ref, slab_ref, *scratch):
    i = pl.program_id(0)
    tm = h_ref.shape[0]
    n_rows = tm + HALO
    n_vrow = n_rows // SUBLANES
    n_slab = D_MODEL // LANES
    n_chunks = D_FF // FF_CHUNK

    for c in range(n_slab):
        lanes = slice(c * LANES, (c + 1) * LANES)
        slab_ref[c, 0:tm, :] = h_ref[:, lanes]
        slab_ref[c, tm:n_rows, :] = halo_ref[:, lanes]
    h = jnp.concatenate(
        [jnp.concatenate([slab_ref[c, pl.ds(j, SUBLANES, stride=n_vrow), :] for c in range(n_slab)], axis=1)
         for j in range(n_vrow)], axis=0)
    hg = h * g2_ref[...]
    hn_first = hg.astype(BF16)
    inv = lax.rsqrt(jnp.mean(h * h, axis=-1, keepdims=True) + EPS)
    hn = (hg * inv).astype(BF16)

    def pair(ref, j, rows=slice(None)):
        return jnp.concatenate([ref[rows, j * FF_CHUNK:(j + 1) * FF_CHUNK],
                                ref[rows, D_FF + j * FF_CHUNK:D_FF + (j + 1) * FF_CHUNK]], axis=1)

    def up_proj(j):
        if j == 0:
            return jnp.dot(hn_first, pair(wup_ref, j), preferred_element_type=F32) * inv
        return jnp.dot(hn, pair(wup_ref, j), preferred_element_type=F32)

    def earlier(x, k):
        cut = (n_vrow - k) * SUBLANES
        wrapped = [pltpu.roll(x[cut + r * SUBLANES:cut + (r + 1) * SUBLANES], 1, axis=0) for r in range(k)]
        return jnp.concatenate(wrapped + [x[:cut]], axis=0)

    def activation(up, j):
        c = pair(cb_ref, j) + pair(cw_ref, j, slice(CONV_W - 1, CONV_W)) * up
        for k in range(1, CONV_W):
            tap = CONV_W - 1 - k
            c = c + pair(cw_ref, j, slice(tap, tap + 1)) * earlier(up, k)
        a, bv = c[:, :FF_CHUNK], c[:, FF_CHUNK:]
        return (a * _sigmoid(a) * bv).astype(BF16)

    def down_proj(acts, j_end):
        k1 = j_end * FF_CHUNK
        lhs = acts[0] if len(acts) == 1 else jnp.concatenate(acts, axis=1)
        return jnp.dot(lhs, wdn_ref[k1 - lhs.shape[1]:k1, :], preferred_element_type=F32)

    group_ends = []
    for size in FF_GROUPS:
        group_ends.append((group_ends[-1] if group_ends else 0) + size)
    assert group_ends[-1] == n_chunks

    acc = h
    up = up_proj(0)
    acts, pending = [], None
    for j in range(n_chunks):
        up_next = up_proj(j + 1) if j + 1 < n_chunks else None
        if pending is not None:
            acc = acc + down_proj(*pending)
            pending = None
        acts.append(activation(up, j))
        up = up_next
        if j + 1 in group_ends:
            pending, acts = (acts, j + 1), []
    acc = acc + down_proj(*pending)

    res = _rms(acc, gf_ref[...]) if last_layer else acc
    for j in range(n_vrow):
        for c in range(n_slab):
            slab_ref[c, pl.ds(j, SUBLANES, stride=n_vrow), :] = res[j * SUBLANES:(j + 1) * SUBLANES,
                                                                     c * LANES:(c + 1) * LANES]

    if not last_layer:
        row = _row_in_seq(i, tiles_per_seq, (tm, LANES))
        for c in range(n_slab):
            out_ref[:, c * LANES:(c + 1) * LANES] = jnp.where(row >= PAD, slab_ref[c, 0:tm, :], 0.0)
        return

    obuf_ref, sem_ref = scratch
    n_steps = pl.num_programs(0)

    def tile_copy(step, head_tile):
        slot, b, t = step % 2, step // tiles_per_seq, step % tiles_per_seq
        if head_tile:
            src = obuf_ref.at[slot, pl.ds(LEAD, tm - LEAD)]
            dst = out_ref.at[pl.ds(b * seq, tm - LEAD)]
        else:
            src = obuf_ref.at[slot]
            dst = out_ref.at[pl.ds(b * seq + t * tm - LEAD, tm)]
        return pltpu.make_async_copy(src, dst, sem_ref.at[slot])

    def for_tile(step, action):
        head_tile = step % tiles_per_seq == 0

        @pl.when(head_tile)
        def _():
            action(tile_copy(step, True))

        @pl.when(jnp.logical_not(head_tile))
        def _():
            action(tile_copy(step, False))

    @pl.when(i >= 2)
    def _():
        for_tile(i - 2, lambda cp: cp.wait())

    for c in range(n_slab):
        obuf_ref[i % 2, :, c * LANES:(c + 1) * LANES] = slab_ref[c, 0:tm, :]
    for_tile(i, lambda cp: cp.start())

    @pl.when(i == n_steps - 1)
    def _():
        @pl.when(i >= 1)
        def _():
            for_tile(i - 1, lambda cp: cp.wait())
        for_tile(i, lambda cp: cp.wait())


def _ffn(h, g2, wup, cw, cb, wdn, gf, tiles_per_seq, seq, layer, last_layer):
    ntok = h.shape[0]
    tok = pl.BlockSpec((TM, D_MODEL), lambda i: (i, 0))
    halo = pl.BlockSpec((HALO, D_MODEL), lambda i: (jnp.maximum(i * (TM // HALO) - 1, 0), 0))
    slab = pltpu.VMEM((D_MODEL // LANES, TM + HALO, LANES), F32)
    if last_layer:
        out_rows = ntok // (LEAD + seq) * seq
        out_spec = pl.BlockSpec(memory_space=pl.ANY)
        scratch = [slab, pltpu.VMEM((2, TM, D_MODEL), F32), pltpu.SemaphoreType.DMA((2,))]
    else:
        out_rows, out_spec, scratch = ntok, tok, [slab]
    return pl.pallas_call(
        functools.partial(_ffn_kernel, tiles_per_seq, seq, last_layer),
        out_shape=jax.ShapeDtypeStruct((out_rows, D_MODEL), F32),
        grid=(ntok // TM,),
        in_specs=[tok, halo, _const_spec((1, D_MODEL)), _layer_spec((D_MODEL, 2 * D_FF), layer),
                  _const_spec((CONV_W, 2 * D_FF)), _const_spec((1, 2 * D_FF)), _layer_spec((D_FF, D_MODEL), layer),
                  _const_spec((1, D_MODEL))],
        out_specs=out_spec,
        scratch_shapes=scratch,
        compiler_params=pltpu.CompilerParams(dimension_semantics=("arbitrary",),
                                             vmem_limit_bytes=VMEM_LIMIT),
        name="ffn",
    )(h, h, g2, wup, cw, cb, wdn, gf)


def _regroup_kernel(wt_ref, out_ref):
    def put(col0, row0):
        out_ref[:, col0:col0 + LANES] = wt_ref[row0:row0 + LANES, :].T.astype(BF16)

    for j in range(_R0 // LANES):
        put(j * LANES, j * LANES)
    for j in range((_GLR0 - _R0) // LANES):
        put(_R0 + j * LANES, _R0 + GLA_RANK + j * LANES)
    tail = wt_ref[_R0:_R0 + LANES, :].T
    lane = lax.broadcasted_iota(jnp.int32, tail.shape, 1)
    out_ref[:, _GLR0:] = jnp.where(lane < GLA_RANK, tail, 0.0).astype(BF16)


def _regroup_inproj(w_in):
    depth, d, cols = w_in.shape
    return pl.pallas_call(
        _regroup_kernel,
        out_shape=jax.ShapeDtypeStruct((depth, d, IN_COLS), BF16),
        grid=(depth,),
        in_specs=[pl.BlockSpec((None, cols, d), lambda l: (l, 0, 0), pipeline_mode=pl.Buffered(1))],
        out_specs=pl.BlockSpec((None, d, IN_COLS), lambda l: (l, 0, 0)),
        compiler_params=pltpu.CompilerParams(dimension_semantics=("arbitrary",), vmem_limit_bytes=VMEM_LIMIT),
        name="regroup_inproj",
    )(jnp.swapaxes(w_in, 1, 2))


def kernel(x, meta_tokens, norm1_g, w_in, w_gk, b_gk, gla_norm_g, w_a, w_pool_grp, pool_scale,
           w_b, b_gates, w_o, norm2_g, w_up, conv_w, conv_b, w_down, final_norm_g):
    batch, seq, d = x.shape
    depth = w_in.shape[0]
    lp = LEAD + seq
    assert d == D_MODEL and lp % TM == 0 and TM % LEAD == 0 and LEAD % CHUNK == 0 and seq % LEAD == 0
    tiles_per_seq = lp // TM
    ntok = batch * lp

    head = jnp.concatenate([jnp.zeros((PAD, d), F32), meta_tokens.astype(F32)], axis=0)
    h = x.astype(F32).reshape(batch * seq, d)

    assert w_in.shape[2] == _GLR0 + GLA_RANK and _R0 == 2 * GLA_DK + GLA_DV
    w_r = _regroup_inproj(w_in)
    wgk_pad = jnp.concatenate(
        [w_gk.astype(BF16), jnp.zeros((depth, LANES - GLA_RANK, GLA_DK), BF16)], axis=1)
    w_a16, w_b16, w_o16, w_pool16 = (w.astype(BF16) for w in (w_a, w_b, w_o, w_pool_grp))
    w_up16, w_down16 = w_up.astype(BF16), w_down.astype(BF16)

    for l in range(depth):
        q, k, v, g, sr, pooled, ga, gb = _inproj(
            h, head, norm1_g[l][None, :], w_r, wgk_pad, b_gk[l][None, :], b_gates[l][None, :],
            ntok, tiles_per_seq, l, seq)
        o = _gla(q, k, v, g, batch, tiles_per_seq)
        h = _merge(o, sr, gla_norm_g[l][None, :], pooled, ga, gb, h, head, w_a16, w_pool16,
                   pool_scale[l][None, :], w_b16, w_o16, tiles_per_seq, l, seq)
        h = _ffn(h, norm2_g[l][None, :], w_up16, conv_w[l], conv_b[l][None, :], w_down16,
                 final_norm_g[None, :], tiles_per_seq, seq, l, l == depth - 1)

    return h.reshape(batch, seq, d)
```

```python
import functools

import jax
import jax.numpy as jnp
from jax import lax
from jax.experimental import pallas as pl
from jax.experimental.pallas import tpu as pltpu

F32 = jnp.float32
BF16 = jnp.bfloat16

D_MODEL = 1024
N_META = 16
GLA_HEADS = 4
GLA_HK = 128
GLA_HV = 256
GLA_DK = GLA_HEADS * GLA_HK
GLA_DV = GLA_HEADS * GLA_HV
GLA_RANK = 16
GLA_TAU = 16.0
CHUNK = 64
POOL_WINDOWS = (2, 4, 8, 16)
POOL_GDIM = 256
D_FF = 2816
CONV_W = 3
EPS = 1e-6

PAD = 240
HALO = 16
TM = 768
LEAD = PAD + N_META
PARTS = TM // LEAD
INPROJ_COLS = 1024
FF_CHUNK = 256
FF_GROUPS = (4, 4, 2, 1)
LANES = 128
SUBLANES = 8
VMEM_LIMIT = 60 * 1024 * 1024

_Q0, _K0, _V0, _R0, _U0, _GA0, _GB0, _GLR0 = 0, 512, 1024, 2048, 3072, 4096, 5120, 6144
IN_COLS = _GLR0 + LANES


def _rms(x, g):
    ms = jnp.mean(x * x, axis=-1, keepdims=True)
    return x * lax.rsqrt(ms + EPS) * g


def _sigmoid(x):
    return 0.5 * jnp.tanh(0.5 * x) + 0.5


def _row_in_seq(tile_idx, tiles_per_seq, shape):
    base = (tile_idx % tiles_per_seq) * shape[0]
    return base + lax.broadcasted_iota(jnp.int32, shape, 0)


def _const_spec(shape):
    nd = len(shape)
    return pl.BlockSpec(shape, lambda *_: (0,) * nd, pipeline_mode=pl.Buffered(1))


def _layer_spec(shape, layer):
    nd = len(shape)
    return pl.BlockSpec((None,) + tuple(shape), lambda *_: (layer,) + (0,) * nd, pipeline_mode=pl.Buffered(1))


def _resid_specs(first_layer, tiles_per_seq, seq):
    def spec(p):
        if first_layer:
            def imap(i):
                b, t = i // tiles_per_seq, i % tiles_per_seq
                return (jnp.maximum(b * (seq // LEAD) + PARTS * t - 1 + p, 0), 0)
        else:
            imap = lambda i: (PARTS * i + p, 0)
        return pl.BlockSpec((LEAD, D_MODEL), imap)
    return [spec(p) for p in range(PARTS)]


def _load_resid(part_refs, head_ref, first_layer, tile_in_seq):
    parts = [r[...] for r in part_refs]
    if first_layer:
        parts[0] = jnp.where(tile_in_seq == 0, head_ref[...], parts[0])
    return jnp.concatenate(parts, axis=0)


def _inproj_kernel(tiles_per_seq, first_layer, h0_ref, h1_ref, h2_ref, head_ref, g1_ref, w_ref, wgk_ref, bgk_ref,
                   bg_ref, q_ref, k_ref, v_ref, g_ref, sr_ref, pooled_ref, ga_ref, gb_ref, carry_ref):
    i = pl.program_id(0)
    tm = TM

    @pl.when(i == 0)
    def _():
        carry_ref[...] = jnp.zeros_like(carry_ref)

    h = _load_resid((h0_ref, h1_ref, h2_ref), head_ref, first_layer, i % tiles_per_seq)
    hn = (h * g1_ref[...]).astype(BF16)
    inv = lax.rsqrt(jnp.mean(h * h, axis=-1, keepdims=True) + EPS)

    half = INPROJ_COLS
    head_pos = (i % tiles_per_seq) * tm + lax.broadcasted_iota(jnp.int32, (HALO, POOL_GDIM), 0)
    stash = {}

    def proj(c0, c1):
        return lambda: jnp.dot(hn, w_ref[:, c0:c1], preferred_element_type=F32) * inv

    def pool_epilogue(first_group):
        def fn(u):
            c0 = first_group * POOL_GDIM
            ext = jnp.concatenate([carry_ref[:, c0:c0 + half], u], axis=0)
            carry_ref[:, c0:c0 + half] = u[tm - HALO:, :]
            for gl in range(half // POOL_GDIM):
                w = POOL_WINDOWS[first_group + gl]
                loc = slice(gl * POOL_GDIM, (gl + 1) * POOL_GDIM)
                s = ext[:, loc]
                sh = 1
                while sh < w:
                    s = s + pltpu.roll(s, sh, axis=0)
                    sh *= 2
                out_cols = slice(c0 + gl * POOL_GDIM, c0 + (gl + 1) * POOL_GDIM)
                mean = s[HALO:, :] * (1.0 / w)
                pooled_ref[:, out_cols] = (mean - u[:, loc]).astype(BF16)
                cnt = jnp.clip(head_pos + 1, 1, w).astype(F32)
                pooled_ref[PAD:PAD + HALO, out_cols] = (
                    s[HALO + PAD:2 * HALO + PAD, :] / cnt - u[PAD:PAD + HALO, loc]).astype(BF16)
        return fn

    def glr_epilogue(x):
        stash["glr"] = x.astype(BF16)

    def forget_gate_epilogue(z):
        z = z + bgk_ref[...]
        g_ref[...] = ((jnp.minimum(z, 0.0) - jnp.log(1.0 + jnp.exp(-jnp.abs(z)))) * (1.0 / GLA_TAU)).astype(BF16)

    def gate_epilogue(out_ref, bias0, c0):
        def fn(x):
            out_ref[:, c0:c0 + half] = _sigmoid(x + bg_ref[:, bias0 + c0:bias0 + c0 + half]).astype(BF16)
        return fn

    def silu_epilogue(c0):
        def fn(x):
            sr_ref[:, c0:c0 + half] = (x * _sigmoid(x)).astype(BF16)
        return fn

    def cast_epilogue(out_ref, c0):
        def fn(x):
            out_ref[:, c0:c0 + half] = x.astype(BF16)
        return fn

    assert half == D_MODEL == GLA_DV == 2 * GLA_DK

    def qk_epilogue(x):
        q_ref[...] = (x[:, :GLA_DK] * GLA_HK ** -0.5).astype(BF16)
        k_ref[...] = x[:, GLA_DK:].astype(BF16)

    stages = [
        (proj(_GLR0, IN_COLS), glr_epilogue),
        (proj(_U0, _GA0), pool_epilogue(0)),
        (proj(_GA0, _GB0), gate_epilogue(ga_ref, 0, 0)),
        (lambda: jnp.dot(stash["glr"], wgk_ref[...], preferred_element_type=F32), forget_gate_epilogue),
        (proj(_GB0, _GLR0), gate_epilogue(gb_ref, D_MODEL, 0)),
        (proj(_R0, _U0), silu_epilogue(0)),
        (proj(_V0, _R0), cast_epilogue(v_ref, 0)),
        (proj(_Q0, _V0), qk_epilogue),
    ]

    res = stages[0][0]()
    for idx, (_, epilogue) in enumerate(stages):
        nxt = stages[idx + 1][0]() if idx + 1 < len(stages) else None
        epilogue(res)
        res = nxt


def _inproj(h, head, g1, w_r, wgk_pad, bgk, bgates, ntok, tiles_per_seq, layer, seq):
    first_layer = layer == 0
    tok = lambda n: pl.BlockSpec((TM, n), lambda i: (i, 0))
    out_shape = (
        jax.ShapeDtypeStruct((ntok, GLA_DK), BF16),
        jax.ShapeDtypeStruct((ntok, GLA_DK), BF16),
        jax.ShapeDtypeStruct((ntok, GLA_DV), BF16),
        jax.ShapeDtypeStruct((ntok, GLA_DK), BF16),
        jax.ShapeDtypeStruct((ntok, GLA_DV), BF16),
        jax.ShapeDtypeStruct((ntok, D_MODEL), BF16),
        jax.ShapeDtypeStruct((ntok, D_MODEL), BF16),
        jax.ShapeDtypeStruct((ntok, D_MODEL), BF16),
    )
    return pl.pallas_call(
        functools.partial(_inproj_kernel, tiles_per_seq, first_layer),
        out_shape=out_shape,
        grid=(ntok // TM,),
        in_specs=_resid_specs(first_layer, tiles_per_seq, seq) + [
            _const_spec((LEAD, D_MODEL)), _const_spec((1, D_MODEL)), _layer_spec((D_MODEL, IN_COLS), layer),
            _layer_spec((LANES, GLA_DK), layer), _const_spec((1, GLA_DK)), _const_spec((1, 2 * D_MODEL))],
        out_specs=(tok(GLA_DK), tok(GLA_DK), tok(GLA_DV), tok(GLA_DK), tok(GLA_DV),
                   tok(D_MODEL), tok(D_MODEL), tok(D_MODEL)),
        scratch_shapes=[pltpu.VMEM((HALO, D_MODEL), F32)],
        compiler_params=pltpu.CompilerParams(dimension_semantics=("arbitrary",),
                                             vmem_limit_bytes=VMEM_LIMIT),
        name="inproj",
    )(h, h, h, head, g1, w_r, wgk_pad, bgk, bgates)


def _gla_kernel(q_ref, k_ref, v_ref, g_ref, o_ref, state_ref):
    t = pl.program_id(1)
    tg = q_ref.shape[0]

    @pl.when(t == 0)
    def _():
        state_ref[...] = jnp.zeros_like(state_ref)

    rr = lax.broadcasted_iota(jnp.int32, (CHUNK, CHUNK), 0)
    cc = lax.broadcasted_iota(jnp.int32, (CHUNK, CHUNK), 1)
    causal = rr >= cc
    tril = causal.astype(BF16)

    n_chunks = tg // CHUNK
    rows = lambda c: slice(c * CHUNK, (c + 1) * CHUNK)
    kcols = lambda h: slice(h * GLA_HK, (h + 1) * GLA_HK)
    vcols = lambda h: slice(h * GLA_HV, (h + 1) * GLA_HV)

    b_chunks = [jnp.dot(tril, g_ref[rows(c), :], preferred_element_type=F32) for c in range(n_chunks)]

    q_dec, k_end, dec, att, d_state = {}, {}, {}, {}, {}
    state = [state_ref[h] for h in range(GLA_HEADS)]

    def score_stage(c):
        b = b_chunks[c]
        dec[c] = jnp.exp(b[CHUNK - 1:CHUNK, :])
        grow = jnp.exp(b)
        q_dec[c] = (q_ref[rows(c), :].astype(F32) * grow).astype(BF16)
        k_inv = k_ref[rows(c), :].astype(F32) * (1.0 / grow)
        k_end[c] = (k_inv * dec[c]).astype(BF16)
        k_inv = k_inv.astype(BF16)
        for h in range(GLA_HEADS):
            att[c, h] = lax.dot_general(q_dec[c][:, kcols(h)], k_inv[:, kcols(h)], (((1,), (1,)), ((), ())),
                                        preferred_element_type=F32)

    def increment_stage(c):
        for h in range(GLA_HEADS):
            att[c, h] = jnp.where(causal, att[c, h], 0.0).astype(BF16)
            d_state[c, h] = lax.dot_general(k_end[c][:, kcols(h)], v_ref[rows(c), vcols(h)],
                                            (((0,), (0,)), ((), ())), preferred_element_type=F32)

    def output_stage(c):
        for h in range(GLA_HEADS):
            o_ref[rows(c), vcols(h)] = (
                jnp.dot(att.pop((c, h)), v_ref[rows(c), vcols(h)], preferred_element_type=F32)
                + jnp.dot(q_dec[c][:, kcols(h)], state[h].astype(BF16), preferred_element_type=F32)).astype(BF16)
            dec_col = jnp.broadcast_to(dec[c][:, kcols(h)], (GLA_HK, GLA_HK)).T
            state[h] = jnp.concatenate([dec_col, dec_col], axis=1) * state[h] + d_state.pop((c, h))

    for step in range(n_chunks + 2):
        if step < n_chunks:
            score_stage(step)
        if 0 <= step - 1 < n_chunks:
            increment_stage(step - 1)
        if 0 <= step - 2 < n_chunks:
            output_stage(step - 2)
    for h in range(GLA_HEADS):
        state_ref[h] = state[h]


def _gla(q, k, v, g, batch, tiles_per_seq):
    ntok = q.shape[0]
    tok = lambda n: pl.BlockSpec((TM, n), lambda b, t: (b * tiles_per_seq + t, 0))
    return pl.pallas_call(
        _gla_kernel,
        out_shape=jax.ShapeDtypeStruct((ntok, GLA_DV), BF16),
        grid=(batch, tiles_per_seq),
        in_specs=[tok(GLA_DK), tok(GLA_DK), tok(GLA_DV), tok(GLA_DK)],
        out_specs=tok(GLA_DV),
        scratch_shapes=[pltpu.VMEM((GLA_HEADS, GLA_HK, GLA_HV), F32)],
        compiler_params=pltpu.CompilerParams(dimension_semantics=("arbitrary", "arbitrary"),
                                             vmem_limit_bytes=VMEM_LIMIT),
        name="gla",
    )(q, k, v, g)


def _merge_kernel(tiles_per_seq, first_layer, o_ref, sr_ref, gn_ref, pooled_ref, ga_ref, gb_ref, h0_ref, h1_ref, h2_ref,
                  head_ref, wa_ref, wp_ref, ps_ref, wb_ref, wo_ref, out_ref):
    i = pl.program_id(0)
    h = _load_resid((h0_ref, h1_ref, h2_ref), head_ref, first_layer, i % tiles_per_seq)
    parts = [jnp.dot(pooled_ref[:, gi * POOL_GDIM:(gi + 1) * POOL_GDIM], wp_ref[gi],
                     preferred_element_type=F32) for gi in range(len(POOL_WINDOWS))]
    yb_in = (jnp.concatenate(parts, axis=1) * ps_ref[...]).astype(BF16)
    y_b = jnp.dot(yb_in, wb_ref[...], preferred_element_type=F32)
    ya = jnp.concatenate(
        [(_rms(o_ref[:, hd * GLA_HV:(hd + 1) * GLA_HV].astype(F32), gn_ref[...])
          * sr_ref[:, hd * GLA_HV:(hd + 1) * GLA_HV].astype(F32)).astype(BF16) for hd in range(GLA_HEADS)], axis=1)
    y_a = jnp.dot(ya, wa_ref[...], preferred_element_type=F32)
    m = ga_ref[...] * y_a.astype(BF16) + gb_ref[...] * y_b.astype(BF16)
    h_new = h + jnp.dot(m, wo_ref[...], preferred_element_type=F32)
    row = _row_in_seq(i, tiles_per_seq, h_new.shape)
    out_ref[...] = jnp.where(row >= PAD, h_new, 0.0)


def _merge(o, sr, gn, pooled, ga, gb, h, head, wa, wp, ps, wb, wo, tiles_per_seq, layer, seq):
    ntok = o.shape[0]
    first_layer = layer == 0
    tok = pl.BlockSpec((TM, D_MODEL), lambda i: (i, 0))
    sq = _layer_spec((D_MODEL, D_MODEL), layer)
    return pl.pallas_call(
        functools.partial(_merge_kernel, tiles_per_seq, first_layer),
        out_shape=jax.ShapeDtypeStruct((ntok, D_MODEL), F32),
        grid=(ntok // TM,),
        in_specs=[tok, tok, _const_spec((1, GLA_HV)), tok, tok, tok] + _resid_specs(first_layer, tiles_per_seq, seq) + [
            _const_spec((LEAD, D_MODEL)), sq,
            _layer_spec((len(POOL_WINDOWS), POOL_GDIM, POOL_GDIM), layer), _const_spec((1, D_MODEL)), sq, sq],
        out_specs=tok,
        compiler_params=pltpu.CompilerParams(dimension_semantics=("arbitrary",),
                                             vmem_limit_bytes=VMEM_LIMIT),
        name="merge",
    )(o, sr, gn, pooled, ga, gb, h, h, h, head, wa, wp, ps, wb, wo)


def _ffn_kernel(tiles_per_seq, seq, last_layer, h_ref, halo_ref, g2_ref, wup_ref, cw_ref, cb_ref, wdn_ref, gf_ref,
                out_ref, slab_ref, *scratch):
    i = pl.program_id(0)
    tm = h_ref.shape[0]
    n_rows = tm + HALO
    n_vrow = n_rows // SUBLANES
    n_slab = D_MODEL // LANES
    n_chunks = D_FF // FF_CHUNK

    for c in range(n_slab):
        lanes = slice(c * LANES, (c + 1) * LANES)
        slab_ref[c, 0:tm, :] = h_ref[:, lanes]
        slab_ref[c, tm:n_rows, :] = halo_ref[:, lanes]
    h = jnp.concatenate(
        [jnp.concatenate([slab_ref[c, pl.ds(j, SUBLANES, stride=n_vrow), :] for c in range(n_slab)], axis=1)
         for j in range(n_vrow)], axis=0)
    hg = h * g2_ref[...]
    hn_first = hg.astype(BF16)
    inv = lax.rsqrt(jnp.mean(h * h, axis=-1, keepdims=True) + EPS)
    hn = (hg * inv).astype(BF16)

    def pair(ref, j, rows=slice(None)):
        return jnp.concatenate([ref[rows, j * FF_CHUNK:(j + 1) * FF_CHUNK],
                                ref[rows, D_FF + j * FF_CHUNK:D_FF + (j + 1) * FF_CHUNK]], axis=1)

    def up_proj(j):
        if j == 0:
            return jnp.dot(hn_first, pair(wup_ref, j), preferred_element_type=F32) * inv
        return jnp.dot(hn, pair(wup_ref, j), preferred_element_type=F32)

    def earlier(x, k):
        cut = (n_vrow - k) * SUBLANES
        wrapped = [pltpu.roll(x[cut + r * SUBLANES:cut + (r + 1) * SUBLANES], 1, axis=0) for r in range(k)]
        return jnp.concatenate(wrapped + [x[:cut]], axis=0)

    def activation(up, j):
        c = pair(cb_ref, j) + pair(cw_ref, j, slice(CONV_W - 1, CONV_W)) * up
        for k in range(1, CONV_W):
            tap = CONV_W - 1 - k
            c = c + pair(cw_ref, j, slice(tap, tap + 1)) * earlier(up, k)
        a, bv = c[:, :FF_CHUNK], c[:, FF_CHUNK:]
        return (a * _sigmoid(a) * bv).astype(BF16)

    def down_proj(acts, j_end):
        k1 = j_end * FF_CHUNK
        lhs = acts[0] if len(acts) == 1 else jnp.concatenate(acts, axis=1)
        return jnp.dot(lhs, wdn_ref[k1 - lhs.shape[1]:k1, :], preferred_element_type=F32)

    group_ends = []
    for size in FF_GROUPS:
        group_ends.append((group_ends[-1] if group_ends else 0) + size)
    assert group_ends[-1] == n_chunks

    acc = h
    up = up_proj(0)
    acts, pending = [], None
    for j in range(n_chunks):
        up_next = up_proj(j + 1) if j + 1 < n_chunks else None
        if pending is not None:
            acc = acc + down_proj(*pending)
            pending = None
        acts.append(activation(up, j))
        up = up_next
        if j + 1 in group_ends:
            pending, acts = (acts, j + 1), []
    acc = acc + down_proj(*pending)

    res = _rms(acc, gf_ref[...]) if last_layer else acc
    for j in range(n_vrow):
        for c in range(n_slab):
            slab_ref[c, pl.ds(j, SUBLANES, stride=n_vrow), :] = res[j * SUBLANES:(j + 1) * SUBLANES,
                                                                     c * LANES:(c + 1) * LANES]

    if not last_layer:
        row = _row_in_seq(i, tiles_per_seq, (tm, LANES))
        for c in range(n_slab):
            out_ref[:, c * LANES:(c + 1) * LANES] = jnp.where(row >= PAD, slab_ref[c, 0:tm, :], 0.0)
        return

    obuf_ref, sem_ref = scratch
    n_steps = pl.num_programs(0)

    def tile_copy(step, head_tile):
        slot, b, t = step % 2, step // tiles_per_seq, step % tiles_per_seq
        if head_tile:
            src = obuf_ref.at[slot, pl.ds(LEAD, tm - LEAD)]
            dst = out_ref.at[pl.ds(b * seq, tm - LEAD)]
        else:
            src = obuf_ref.at[slot]
            dst = out_ref.at[pl.ds(b * seq + t * tm - LEAD, tm)]
        return pltpu.make_async_copy(src, dst, sem_ref.at[slot])

    def for_tile(step, action):
        head_tile = step % tiles_per_seq == 0

        @pl.when(head_tile)
        def _():
            action(tile_copy(step, True))

        @pl.when(jnp.logical_not(head_tile))
        def _():
            action(tile_copy(step, False))

    @pl.when(i >= 2)
    def _():
        for_tile(i - 2, lambda cp: cp.wait())

    for c in range(n_slab):
        obuf_ref[i % 2, :, c * LANES:(c + 1) * LANES] = slab_ref[c, 0:tm, :]
    for_tile(i, lambda cp: cp.start())

    @pl.when(i == n_steps - 1)
    def _():
        @pl.when(i >= 1)
        def _():
            for_tile(i - 1, lambda cp: cp.wait())
        for_tile(i, lambda cp: cp.wait())


def _ffn(h, g2, wup, cw, cb, wdn, gf, tiles_per_seq, seq, layer, last_layer):
    ntok = h.shape[0]
    tok = pl.BlockSpec((TM, D_MODEL), lambda i: (i, 0))
    halo = pl.BlockSpec((HALO, D_MODEL), lambda i: (jnp.maximum(i * (TM // HALO) - 1, 0), 0))
    slab = pltpu.VMEM((D_MODEL // LANES, TM + HALO, LANES), F32)
    if last_layer:
        out_rows = ntok // (LEAD + seq) * seq
        out_spec = pl.BlockSpec(memory_space=pl.ANY)
        scratch = [slab, pltpu.VMEM((2, TM, D_MODEL), F32), pltpu.SemaphoreType.DMA((2,))]
    else:
        out_rows, out_spec, scratch = ntok, tok, [slab]
    return pl.pallas_call(
        functools.partial(_ffn_kernel, tiles_per_seq, seq, last_layer),
        out_shape=jax.ShapeDtypeStruct((out_rows, D_MODEL), F32),
        grid=(ntok // TM,),
        in_specs=[tok, halo, _const_spec((1, D_MODEL)), _layer_spec((D_MODEL, 2 * D_FF), layer),
                  _const_spec((CONV_W, 2 * D_FF)), _const_spec((1, 2 * D_FF)), _layer_spec((D_FF, D_MODEL), layer),
                  _const_spec((1, D_MODEL))],
        out_specs=out_spec,
        scratch_shapes=scratch,
        compiler_params=pltpu.CompilerParams(dimension_semantics=("arbitrary",),
                                             vmem_limit_bytes=VMEM_LIMIT),
        name="ffn",
    )(h, h, g2, wup, cw, cb, wdn, gf)


def _regroup_kernel(wt_ref, out_ref):
    def put(col0, row0):
        out_ref[:, col0:col0 + LANES] = wt_ref[row0:row0 + LANES, :].T.astype(BF16)

    for j in range(_R0 // LANES):
        put(j * LANES, j * LANES)
    for j in range((_GLR0 - _R0) // LANES):
        put(_R0 + j * LANES, _R0 + GLA_RANK + j * LANES)
    tail = wt_ref[_R0:_R0 + LANES, :].T
    lane = lax.broadcasted_iota(jnp.int32, tail.shape, 1)
    out_ref[:, _GLR0:] = jnp.where(lane < GLA_RANK, tail, 0.0).astype(BF16)


def _regroup_inproj(w_in):
    depth, d, cols = w_in.shape
    return pl.pallas_call(
        _regroup_kernel,
        out_shape=jax.ShapeDtypeStruct((depth, d, IN_COLS), BF16),
        grid=(depth,),
        in_specs=[pl.BlockSpec((None, cols, d), lambda l: (l, 0, 0), pipeline_mode=pl.Buffered(1))],
        out_specs=pl.BlockSpec((None, d, IN_COLS), lambda l: (l, 0, 0)),
        compiler_params=pltpu.CompilerParams(dimension_semantics=("arbitrary",), vmem_limit_bytes=VMEM_LIMIT),
        name="regroup_inproj",
    )(jnp.swapaxes(w_in, 1, 2))


def kernel(x, meta_tokens, norm1_g, w_in, w_gk, b_gk, gla_norm_g, w_a, w_pool_grp, pool_scale,
           w_b, b_gates, w_o, norm2_g, w_up, conv_w, conv_b, w_down, final_norm_g):
    batch, seq, d = x.shape
    depth = w_in.shape[0]
    lp = LEAD + seq
    assert d == D_MODEL and lp % TM == 0 and TM % LEAD == 0 and LEAD % CHUNK == 0 and seq % LEAD == 0
    tiles_per_seq = lp // TM
    ntok = batch * lp

    head = jnp.concatenate([jnp.zeros((PAD, d), F32), meta_tokens.astype(F32)], axis=0)
    h = x.astype(F32).reshape(batch * seq, d)

    assert w_in.shape[2] == _GLR0 + GLA_RANK and _R0 == 2 * GLA_DK + GLA_DV
    w_r = _regroup_inproj(w_in)
    wgk_pad = jnp.concatenate(
        [w_gk.astype(BF16), jnp.zeros((depth, LANES - GLA_RANK, GLA_DK), BF16)], axis=1)
    w_a16, w_b16, w_o16, w_pool16 = (w.astype(BF16) for w in (w_a, w_b, w_o, w_pool_grp))
    w_up16, w_down16 = w_up.astype(BF16), w_down.astype(BF16)

    for l in range(depth):
        q, k, v, g, sr, pooled, ga, gb = _inproj(
            h, head, norm1_g[l][None, :], w_r, wgk_pad, b_gk[l][None, :], b_gates[l][None, :],
            ntok, tiles_per_seq, l, seq)
        o = _gla(q, k, v, g, batch, tiles_per_seq)
        h = _merge(o, sr, gla_norm_g[l][None, :], pooled, ga, gb, h, head, w_a16, w_pool16,
                   pool_scale[l][None, :], w_b16, w_o16, tiles_per_seq, l, seq)
        h = _ffn(h, norm2_g[l][None, :], w_up16, conv_w[l], conv_b[l][None, :], w_down16,
                 final_norm_g[None, :], tiles_per_seq, seq, l, l == depth - 1)

    return h.reshape(batch, seq, d)
```
